```python
import math
import jax, jax.numpy as jnp
from jax import lax
import numpy as np

D_MODEL = 1024
BATCH = 8
SEQ = 2048
DEPTH = 4

EPS = 1e-6
A_GROUPS = 4
A_GROUP_DIM = 128
A_CHUNK = 128
A_WIDTH = A_GROUPS * A_GROUP_DIM
B_WINDOWS = (2, 4, 8, 16)
B_GROUPS = 4
B_GROUP_DIM = 128
B_WIDTH = B_GROUPS * B_GROUP_DIM
C_HEADS = 8
C_HEAD = 64
C_WIDTH = C_HEADS * C_HEAD
C_DECAY_LORA = 64
C_AAA_LORA = 64
C_GATE_LORA = 128
C_LORA = C_DECAY_LORA + C_AAA_LORA + C_GATE_LORA
C_PROJ = 3 * C_WIDTH + C_LORA
C_GN_EPS = 64e-5
D_HEADS = 8
D_HEAD = 64
D_WIDTH = D_HEADS * D_HEAD
D_GROUPS = 2
D_STATE = 128
D_CONV = 4
D_CHUNK = 128
D_CONV_DIM = D_WIDTH + 2 * D_GROUPS * D_STATE
N_BRANCH = 4
BRANCH_WIDTH = 512
IN_SPLITS = (2 * A_WIDTH, B_WIDTH, C_PROJ, D_WIDTH, D_CONV_DIM, D_HEADS, N_BRANCH * D_MODEL)
IN_DIM = sum(IN_SPLITS)
D_FF = 3584
N_EXPERTS = 8
TOP_K = 2
N_DENSE = (DEPTH + 1) // 2
N_MOE = DEPTH // 2

kernel_name = 'hybrid_gated_4mixer_moe_trunk'

F32 = jnp.float32


def split_cols(z, sizes):
    out, start = [], 0
    for s in sizes:
        out.append(z[..., start:start + s])
        start += s
    return out


def rmsnorm(x, g):
    xf = x.astype(F32)
    y = xf * lax.rsqrt(jnp.mean(xf * xf, axis=-1, keepdims=True) + EPS)
    return (y * g).astype(x.dtype)


def token_shift(x):
    return jnp.pad(x, ((0, 0), (1, 0), (0, 0)))[:, :-1]


def causal_depthwise_conv(x, w, b):
    k_w, ch = w.shape
    y = lax.conv_general_dilated(x, w[:, None, :], window_strides=(1,), padding=[(k_w - 1, 0)],
                                 dimension_numbers=('NWC', 'WIO', 'NWC'), feature_group_count=ch)
    return y + b


def mixer_spatial_gating(uv, ln_g, ln_b, w_s, b_s):
    bsz, t_len, _ = uv.shape
    u, v = jnp.split(jax.nn.gelu(uv), 2, axis=-1)
    vf = v.astype(F32)
    mu = jnp.mean(vf, axis=-1, keepdims=True)
    var = jnp.mean(jnp.square(vf - mu), axis=-1, keepdims=True)
    v = ((vf - mu) * lax.rsqrt(var + EPS) * ln_g + ln_b).astype(uv.dtype)
    v = v.reshape(bsz, t_len // A_CHUNK, A_CHUNK, A_GROUPS, A_GROUP_DIM)
    causal = jnp.tril(jnp.ones((A_CHUNK, A_CHUNK), dtype=bool))
    w = jnp.where(causal, w_s, jnp.zeros_like(w_s))
    v = jnp.einsum('gts,bcsge->bctge', w, v) + b_s.T[None, None, :, :, None]
    return u * v.reshape(bsz, t_len, A_WIDTH)


def mixer_pool(p, w_pool, scale):
    bsz, t_len, _ = p.shape
    pf = p.astype(F32)
    cs = jnp.pad(jnp.cumsum(pf, axis=1), ((0, 0), (1, 0), (0, 0)))
    count = jnp.arange(1, t_len + 1, dtype=F32)[None, :, None]
    outs = []
    for gi, win in enumerate(B_WINDOWS):
        sl = slice(gi * B_GROUP_DIM, (gi + 1) * B_GROUP_DIM)
        c = cs[:, :, sl]
        hi = c[:, 1:]
        lo = jnp.pad(c[:, :t_len + 1 - win], ((0, 0), (win - 1, 0), (0, 0)))
        mean = (hi - lo) / jnp.minimum(count, float(win))
        outs.append(mean - pf[:, :, sl])
    pooled = jnp.stack(outs, axis=2).astype(p.dtype)
    y = jnp.einsum('btge,gef->btgf', pooled, w_pool).reshape(bsz, t_len, B_WIDTH)
    return y * scale


def rwkv7_step(state, inp):
    r_t, w_t, k_t, v_t, kk_t, b_t = inp
    s_kk = jnp.einsum('bhvk,bhk->bhv', state, kk_t)
    state = state * w_t[:, :, None, :] - s_kk[..., None] * b_t[:, :, None, :] + v_t[..., None] * k_t[:, :, None, :]
    return state, jnp.einsum('bhvk,bhk->bhv', state, r_t)


def mixer_rwkv7(zc, mix, w0, w2, a0, a2, g2, k_k, k_a, r_k, ln_g, ln_b):
    bsz, t_len, _ = zc.shape
    zc = zc + (token_shift(zc) - zc) * mix
    r, k, v, lw, la, lg = split_cols(zc, (C_WIDTH, C_WIDTH, C_WIDTH, C_DECAY_LORA, C_AAA_LORA, C_GATE_LORA))
    w_log = -jax.nn.softplus(-(w0 + jnp.tanh(lw) @ w2).astype(F32)) - 0.5
    decay = jnp.exp(-jnp.exp(w_log))
    a = jax.nn.sigmoid((a0 + la @ a2).astype(F32))
    g = jax.nn.sigmoid(lg) @ g2
    heads = lambda t: t.astype(F32).reshape(bsz, t_len, C_HEADS, C_HEAD)
    r, k, v, a, decay = heads(r), heads(k), heads(v), heads(a), heads(decay)
    kk = k * k_k.astype(F32).reshape(C_HEADS, C_HEAD)
    kk = kk / jnp.maximum(jnp.linalg.norm(kk, axis=-1, keepdims=True), 1e-12)
    k = k * (1.0 + (a - 1.0) * k_a.astype(F32).reshape(C_HEADS, C_HEAD))
    tm = lambda t: jnp.moveaxis(t, 1, 0)
    s0 = jnp.zeros((bsz, C_HEADS, C_HEAD, C_HEAD), F32)
    _, o = lax.scan(rwkv7_step, s0, (tm(r), tm(decay), tm(k), tm(v), tm(kk), tm(kk * a)))
    o = jnp.moveaxis(o, 0, 1)
    mu = jnp.mean(o, axis=-1, keepdims=True)
    var = jnp.mean(jnp.square(o - mu), axis=-1, keepdims=True)
    o = ((o - mu) * lax.rsqrt(var + C_GN_EPS)).reshape(bsz, t_len, C_WIDTH) * ln_g + ln_b
    bonus = jnp.sum(r * k * r_k.astype(F32), axis=-1, keepdims=True) * v
    o = o + bonus.reshape(bsz, t_len, C_WIDTH)
    return (o * g).astype(zc.dtype)


def ssd_chunked(xh, dt, a_head, bm, cm):
    bsz, t_len, n_h, p_dim = xh.shape
    nc, lc, ng, ne, ns = t_len // D_CHUNK, D_CHUNK, D_GROUPS, n_h // D_GROUPS, D_STATE
    xd = (xh * dt[..., None]).reshape(bsz, nc, lc, ng, ne, p_dim)
    a = (dt * a_head).reshape(bsz, nc, lc, ng, ne).transpose(0, 3, 4, 1, 2)
    bc = bm.reshape(bsz, nc, lc, ng, ns)
    cc = cm.reshape(bsz, nc, lc, ng, ns)
    a_cs = jnp.cumsum(a, axis=-1)
    causal = jnp.tril(jnp.ones((lc, lc), dtype=bool))
    seg = a_cs[..., :, None] - a_cs[..., None, :]
    decay_in = jnp.exp(jnp.where(causal, seg, -jnp.inf))
    cb = jnp.einsum('bclgn,bcsgn->bgcls', cc, bc)
    y_diag = jnp.einsum('bgcls,bgecls,bcsgep->bclgep', cb, decay_in, xd)
    decay_to_end = jnp.exp(a_cs[..., -1:] - a_cs)
    states = jnp.einsum('bclgn,bgecl,bclgep->cbgepn', bc, decay_to_end, xd)
    chunk_decay = jnp.exp(a_cs[..., -1]).transpose(3, 0, 1, 2)

    def carry_step(h, inp):
        s_c, d_c = inp
        return h * d_c[..., None, None] + s_c, h

    h0 = jnp.zeros(states.shape[1:], states.dtype)
    _, prev = lax.scan(carry_step, h0, (states, chunk_decay))
    y_off = jnp.einsum('bclgn,cbgepn,bgecl->bclgep', cc, prev, jnp.exp(a_cs))
    return (y_diag + y_off).reshape(bsz, t_len, n_h, p_dim)


def mixer_ssd(zg, xbc, dt_raw, conv_w, conv_b, dt_bias, a_log, d_skip, norm_g):
    bsz, t_len, _ = xbc.shape
    xbc = jax.nn.silu(causal_depthwise_conv(xbc, conv_w, conv_b))
    xs, bm, cm = split_cols(xbc, (D_WIDTH, D_GROUPS * D_STATE, D_GROUPS * D_STATE))
    xh = xs.astype(F32).reshape(bsz, t_len, D_HEADS, D_HEAD)
    dt = jax.nn.softplus(dt_raw.astype(F32) + dt_bias)
    a_head = -jnp.exp(a_log.astype(F32))
    y = ssd_chunked(xh, dt, a_head,
                    bm.astype(F32).reshape(bsz, t_len, D_GROUPS, D_STATE),
                    cm.astype(F32).reshape(bsz, t_len, D_GROUPS, D_STATE))
    y = y + d_skip[:, None] * xh
    y = y.reshape(bsz, t_len, D_WIDTH) * jax.nn.silu(zg.astype(F32))
    return rmsnorm(y, norm_g).astype(zg.dtype)


def swiglu(h, wg, wu, wd):
    return (jax.nn.silu(h @ wg) * (h @ wu)) @ wd


def moe_swiglu(h, router, wg, wu, wd):
    logits = jnp.einsum('btd,de->bte', h, router).astype(F32)
    top_v, top_i = lax.top_k(logits, TOP_K)
    probs = jax.nn.softmax(top_v, axis=-1)
    combine = jnp.sum(jax.nn.one_hot(top_i, N_EXPERTS, dtype=F32) * probs[..., None], axis=-2)
    out = jnp.zeros_like(h)
    for e in range(N_EXPERTS):
        out = out + swiglu(h, wg[e], wu[e], wd[e]) * combine[..., e:e + 1].astype(h.dtype)
    return out


def setup_inputs(seed: int = 0) -> dict:
    key = jax.random.key(seed)
    ks = iter(jax.random.split(key, 48))
    nrm = lambda shape, s: jax.random.normal(next(ks), shape, F32) * s
    L = DEPTH
    x = nrm((BATCH, SEQ, D_MODEL), 1.0)
    norm_mix_g = 1.0 + nrm((L, D_MODEL), 0.02)
    w_in = nrm((L, D_MODEL, IN_DIM), D_MODEL ** -0.5)
    a_ln_g = 1.0 + nrm((L, A_WIDTH), 0.02)
    a_ln_b = nrm((L, A_WIDTH), 0.02)
    a_w_s = nrm((L, A_GROUPS, A_CHUNK, A_CHUNK), A_CHUNK ** -0.5)
    a_b_s = 1.0 + nrm((L, A_GROUPS, A_CHUNK), 0.1)
    b_w_pool = nrm((L, B_GROUPS, B_GROUP_DIM, B_GROUP_DIM), B_GROUP_DIM ** -0.5)
    b_scale = 1.0 + nrm((L, B_WIDTH), 0.1)
    c_mix = jax.random.uniform(next(ks), (L, C_PROJ), F32)
    c_w0 = jax.random.uniform(next(ks), (L, C_WIDTH), F32, -6.5, -1.5)
    c_w2 = nrm((L, C_DECAY_LORA, C_WIDTH), 0.1)
    c_a0 = nrm((L, C_WIDTH), 0.1)
    c_a2 = nrm((L, C_AAA_LORA, C_WIDTH), 0.1)
    c_g2 = nrm((L, C_GATE_LORA, C_WIDTH), C_GATE_LORA ** -0.5)
    c_k_k = 0.85 + nrm((L, C_WIDTH), 0.02)
    c_k_a = 1.0 + nrm((L, C_WIDTH), 0.02)
    c_r_k = nrm((L, C_HEADS, C_HEAD), 0.1)
    c_ln_g = 1.0 + nrm((L, C_WIDTH), 0.02)
    c_ln_b = nrm((L, C_WIDTH), 0.02)
    d_conv_w = nrm((L, D_CONV, D_CONV_DIM), D_CONV ** -0.5)
    d_conv_b = nrm((L, D_CONV_DIM), 0.02)
    dt0 = jnp.exp(jax.random.uniform(next(ks), (L, D_HEADS), F32, math.log(1e-3), math.log(1e-1)))
    d_dt_bias = dt0 + jnp.log(-jnp.expm1(-dt0))
    d_a_log = jnp.log(jax.random.uniform(next(ks), (L, D_HEADS), F32, 1.0, 16.0))
    d_skip = 1.0 + nrm((L, D_HEADS), 0.1)
    d_norm_g = 1.0 + nrm((L, D_WIDTH), 0.02)
    w_branch = nrm((L, N_BRANCH, BRANCH_WIDTH, D_MODEL), BRANCH_WIDTH ** -0.5)
    w_out = nrm((L, D_MODEL, D_MODEL), D_MODEL ** -0.5)
    norm_ffn_g = 1.0 + nrm((L, D_MODEL), 0.02)
    ffn_w_gate = nrm((N_DENSE, D_MODEL, D_FF), D_MODEL ** -0.5)
    ffn_w_up = nrm((N_DENSE, D_MODEL, D_FF), D_MODEL ** -0.5)
    ffn_w_down = nrm((N_DENSE, D_FF, D_MODEL), D_FF ** -0.5)
    moe_router = nrm((N_MOE, D_MODEL, N_EXPERTS), D_MODEL ** -0.5)
    moe_w_gate = nrm((N_MOE, N_EXPERTS, D_MODEL, D_FF), D_MODEL ** -0.5)
    moe_w_up = nrm((N_MOE, N_EXPERTS, D_MODEL, D_FF), D_MODEL ** -0.5)
    moe_w_down = nrm((N_MOE, N_EXPERTS, D_FF, D_MODEL), D_FF ** -0.5)
    norm_final_g = 1.0 + nrm((D_MODEL,), 0.02)
    return {'x': x, 'norm_mix_g': norm_mix_g, 'w_in': w_in,
            'a_ln_g': a_ln_g, 'a_ln_b': a_ln_b, 'a_w_s': a_w_s, 'a_b_s': a_b_s,
            'b_w_pool': b_w_pool, 'b_scale': b_scale,
            'c_mix': c_mix, 'c_w0': c_w0, 'c_w2': c_w2, 'c_a0': c_a0, 'c_a2': c_a2, 'c_g2': c_g2,
            'c_k_k': c_k_k, 'c_k_a': c_k_a, 'c_r_k': c_r_k, 'c_ln_g': c_ln_g, 'c_ln_b': c_ln_b,
            'd_conv_w': d_conv_w, 'd_conv_b': d_conv_b, 'd_dt_bias': d_dt_bias, 'd_a_log': d_a_log,
            'd_skip': d_skip, 'd_norm_g': d_norm_g,
            'w_branch': w_branch, 'w_out': w_out, 'norm_ffn_g': norm_ffn_g,
            'ffn_w_gate': ffn_w_gate, 'ffn_w_up': ffn_w_up, 'ffn_w_down': ffn_w_down,
            'moe_router': moe_router, 'moe_w_gate': moe_w_gate, 'moe_w_up': moe_w_up, 'moe_w_down': moe_w_down,
            'norm_final_g': norm_final_g}


def reference(x, norm_mix_g, w_in, a_ln_g, a_ln_b, a_w_s, a_b_s, b_w_pool, b_scale,
              c_mix, c_w0, c_w2, c_a0, c_a2, c_g2, c_k_k, c_k_a, c_r_k, c_ln_g, c_ln_b,
              d_conv_w, d_conv_b, d_dt_bias, d_a_log, d_skip, d_norm_g,
              w_branch, w_out, norm_ffn_g, ffn_w_gate, ffn_w_up, ffn_w_down,
              moe_router, moe_w_gate, moe_w_up, moe_w_down, norm_final_g):
    bsz, t_len, _ = x.shape
    for l in range(DEPTH):
        h = rmsnorm(x, norm_mix_g[l])
        z = h @ w_in[l]
        z_a, z_b, z_c, z_dz, z_dxbc, z_ddt, z_g = split_cols(z, IN_SPLITS)
        y_a = mixer_spatial_gating(z_a, a_ln_g[l], a_ln_b[l], a_w_s[l], a_b_s[l])
        y_b = mixer_pool(z_b, b_w_pool[l], b_scale[l])
        y_c = mixer_rwkv7(z_c, c_mix[l], c_w0[l], c_w2[l], c_a0[l], c_a2[l], c_g2[l],
                          c_k_k[l], c_k_a[l], c_r_k[l], c_ln_g[l], c_ln_b[l])
        y_d = mixer_ssd(z_dz, z_dxbc, z_ddt, d_conv_w[l], d_conv_b[l], d_dt_bias[l],
                        d_a_log[l], d_skip[l], d_norm_g[l])
        branches = jnp.stack([y_a, y_b, y_c, y_d], axis=2)
        proj = jnp.einsum('btkc,kcd->btkd', branches, w_branch[l])
        gates = jax.nn.sigmoid(z_g.astype(F32)).reshape(bsz, t_len, N_BRANCH, D_MODEL).astype(x.dtype)
        merged = jnp.sum(gates * proj, axis=2)
        x = x + merged @ w_out[l]
        h = rmsnorm(x, norm_ffn_g[l])
        if l % 2 == 0:
            i = l // 2
            x = x + swiglu(h, ffn_w_gate[i], ffn_w_up[i], ffn_w_down[i])
        else:
            i = l // 2
            x = x + moe_swiglu(h, moe_router[i], moe_w_gate[i], moe_w_up[i], moe_w_down[i])
    return rmsnorm(x, norm_final_g)
```

```python
import functools

import jax
import jax.numpy as jnp
from jax import lax
from jax.experimental import pallas as pl
from jax.experimental.pallas import tpu as pltpu

F32 = jnp.float32
BF16 = jnp.bfloat16
HIGHEST = lax.Precision.HIGHEST

EPS = 1e-6
GN_EPS = 64e-5
D_MODEL = 1024
N_HEADS = 8
HEAD = 64
WIDTH = 512
A_CHUNK = 128
D_CHUNK = 128
D_STATE = 128
C_CHUNK = 64
B_WINDOWS = (2, 4, 8, 16)
N_EXPERTS = 8
D_FF = 3584

Z_G, Z_A, Z_DXBC, Z_B, Z_R, Z_K, Z_V, Z_DZ, Z_LORA, Z_DT = (
    0, 4096, 5120, 6144, 6656, 7168, 7680, 8192, 8704, 8960)
Z_WIDTH = 9216

TM_PROJ = 1024
TN_PROJ = 1024
TM_A = 512
TM_B = 512
TM_C = 256
TM_D = 128
TM_MERGE = 256
TM_FFN = 1024
TM_MOE = 512
TF_FFN = 512
TM_ROUTE = 512
TM_GATHER = 256
TM_COMBINE = 256
VMEM_LIMIT = 48 * 1024 * 1024


def _cp(*sem):
    return pltpu.CompilerParams(dimension_semantics=sem, vmem_limit_bytes=VMEM_LIMIT)


def _bdot(a, b):
    return jnp.dot(a.astype(BF16), b.astype(BF16), preferred_element_type=F32)


def _fdot(a, b):
    return jnp.dot(a, b, preferred_element_type=F32, precision=HIGHEST)


def _dot_nt(a, b, precision=None):
    return lax.dot_general(a, b, (((1,), (1,)), ((), ())), preferred_element_type=F32,
                           precision=precision)


def _dot_tn(a, b, precision=None):
    return lax.dot_general(a, b, (((0,), (0,)), ((), ())), preferred_element_type=F32,
                           precision=precision)


def _sigmoid(x):
    return 1.0 / (1.0 + jnp.exp(-x))


def _silu(x):
    return x * _sigmoid(x)


def _softplus(x):
    return jnp.maximum(x, 0.0) + jnp.log(1.0 + jnp.exp(-jnp.abs(x)))


def _rmsnorm(x, g):
    return x * lax.rsqrt(jnp.mean(x * x, axis=-1, keepdims=True) + EPS) * g


def _inproj_body(x_ref, g_ref, w_ref, o_ref, h_ref):
    @pl.when(pl.program_id(1) == 0)
    def _():
        h_ref[...] = _rmsnorm(x_ref[...], g_ref[...]).astype(BF16)

    o_ref[...] = jnp.dot(h_ref[...], w_ref[...], preferred_element_type=F32)


def _in_proj(x2, g, w_all, layer):
    n, d = x2.shape
    zw = w_all.shape[-1]
    tm = min(TM_PROJ, n)
    tn = TN_PROJ
    return pl.pallas_call(
        _inproj_body,
        grid=(n // tm, zw // tn),
        in_specs=[pl.BlockSpec((tm, d), lambda i, j: (i, 0)),
                  pl.BlockSpec((1, d), lambda i, j: (0, 0)),
                  pl.BlockSpec((None, d, tn), lambda i, j: (layer, 0, j))],
        out_specs=pl.BlockSpec((tm, tn), lambda i, j: (i, j)),
        out_shape=jax.ShapeDtypeStruct((n, zw), F32),
        scratch_shapes=[pltpu.VMEM((tm, d), BF16)],
        compiler_params=_cp("parallel", "arbitrary"),
        name="in_proj",
    )(x2, g, w_all)


def _mixa_body(uv_ref, lng_ref, lnb_ref, ws_ref, bs_ref, o_ref):
    tm = uv_ref.shape[0]
    row = lax.broadcasted_iota(jnp.int32, (A_CHUNK, A_CHUNK), 0)
    col = lax.broadcasted_iota(jnp.int32, (A_CHUNK, A_CHUNK), 1)
    ws = [jnp.where(row >= col, ws_ref[g], 0.0).astype(BF16) for g in range(4)]
    for c in range(tm // A_CHUNK):
        rows = slice(c * A_CHUNK, (c + 1) * A_CHUNK)
        uv = jax.nn.gelu(uv_ref[rows, :])
        u = uv[:, :WIDTH]
        v = uv[:, WIDTH:]
        mu = jnp.mean(v, axis=-1, keepdims=True)
        d = v - mu
        var = jnp.mean(d * d, axis=-1, keepdims=True)
        vn = d * lax.rsqrt(var + EPS) * lng_ref[...] + lnb_ref[...]
        for g in range(4):
            cols = slice(g * 128, (g + 1) * 128)
            sg = jnp.dot(ws[g], vn[:, cols].astype(BF16), preferred_element_type=F32)
            sg = sg + bs_ref[:, g:g + 1]
            o_ref[rows, cols] = (u[:, cols] * sg).astype(o_ref.dtype)


def _mixer_a(z, ln_g, ln_b, w_s, b_s_t, seq):
    n = z.shape[0]
    tm = min(TM_A, seq)
    return pl.pallas_call(
        _mixa_body,
        grid=(n // tm,),
        in_specs=[pl.BlockSpec((tm, 2 * WIDTH), lambda i: (i, Z_A // (2 * WIDTH))),
                  pl.BlockSpec((1, WIDTH), lambda i: (0, 0)),
                  pl.BlockSpec((1, WIDTH), lambda i: (0, 0)),
                  pl.BlockSpec((4, A_CHUNK, A_CHUNK), lambda i: (0, 0, 0)),
                  pl.BlockSpec((A_CHUNK, 4), lambda i: (0, 0))],
        out_specs=pl.BlockSpec((tm, WIDTH), lambda i: (i, 0)),
        out_shape=jax.ShapeDtypeStruct((n, WIDTH), BF16),
        compiler_params=_cp("parallel"),
        name="mixer_a",
    )(z, ln_g, ln_b, w_s, b_s_t)


B_PAD = 16


def _mixb_body(p_ref, w_ref, sc_ref, o_ref, pad_ref):
    t = pl.program_id(1)
    tm = p_ref.shape[0]

    @pl.when(t == 0)
    def _():
        pad_ref[0:B_PAD, :] = jnp.zeros((B_PAD, WIDTH), F32)

    @pl.when(t > 0)
    def _():
        pad_ref[0:B_PAD, :] = pad_ref[tm:tm + B_PAD, :]

    pad_ref[B_PAD:B_PAD + tm, :] = p_ref[...]
    pos = (t * tm + lax.broadcasted_iota(jnp.int32, (tm, 1), 0)).astype(F32)
    for g, win in enumerate(B_WINDOWS):
        cols = slice(g * 128, (g + 1) * 128)
        p_g = pad_ref[B_PAD:B_PAD + tm, cols]
        acc = p_g
        for j in range(1, win):
            acc = acc + pad_ref[B_PAD - j:B_PAD - j + tm, cols]
        pooled = acc / jnp.minimum(pos + 1.0, float(win)) - p_g
        y = _bdot(pooled, w_ref[g]) * sc_ref[:, cols]
        o_ref[:, cols] = y.astype(o_ref.dtype)


def _mixer_b(z, w_pool, scale, batch, seq):
    n = z.shape[0]
    tm = min(TM_B, seq)
    nt = seq // tm
    return pl.pallas_call(
        _mixb_body,
        grid=(batch, nt),
        in_specs=[pl.BlockSpec((tm, WIDTH), lambda b, t: (b * nt + t, Z_B // WIDTH)),
                  pl.BlockSpec((4, 128, 128), lambda b, t: (0, 0, 0)),
                  pl.BlockSpec((1, WIDTH), lambda b, t: (0, 0))],
        out_specs=pl.BlockSpec((tm, WIDTH), lambda b, t: (b * nt + t, 0)),
        out_shape=jax.ShapeDtypeStruct((n, WIDTH), BF16),
        scratch_shapes=[pltpu.VMEM((tm + B_PAD, WIDTH), F32)],
        compiler_params=_cp("parallel", "arbitrary"),
        name="mixer_b",
    )(z, w_pool, scale)


D_PAD = 8


def _mixd_body(dz_ref, xbc_ref, dt_ref, cw_ref, cb_ref, dtb_ref, alog_ref, dsk_ref, ng_ref,
               o_ref, pad_ref, h_ref, y_ref):
    t = pl.program_id(1)
    tm = xbc_ref.shape[0]
    L = D_CHUNK

    @pl.when(t == 0)
    def _():
        pad_ref[0:D_PAD, :] = jnp.zeros((D_PAD, 2 * WIDTH), F32)
        h_ref[...] = jnp.zeros(h_ref.shape, F32)

    @pl.when(t > 0)
    def _():
        pad_ref[0:D_PAD, :] = pad_ref[tm:tm + D_PAD, :]

    pad_ref[D_PAD:D_PAD + tm, :] = xbc_ref[...]

    row = lax.broadcasted_iota(jnp.int32, (L, L), 0)
    col = lax.broadcasted_iota(jnp.int32, (L, L), 1)
    causal = row >= col
    tril = jnp.where(causal, 1.0, 0.0).astype(F32)
    left = col < HEAD
    top = row < HEAD
    a_head = -jnp.exp(alog_ref[...])

    for c in range(tm // L):
        base = D_PAD + c * L
        rows = slice(c * L, (c + 1) * L)
        conv = cb_ref[...]
        for k in range(4):
            conv = conv + cw_ref[k:k + 1, :] * pad_ref[base - 3 + k:base - 3 + k + L, :]
        xbc = _silu(conv)
        dt = _softplus(dt_ref[rows, :] + dtb_ref[...])
        cs = _fdot(tril, dt * a_head)
        cs_t = cs.T
        last = cs[L - 1:L, :]
        e_cs = jnp.exp(cs)
        e_end = jnp.exp(last - cs)
        e_last = jnp.exp(last)
        for g in range(2):
            b_g = xbc[:, WIDTH + g * D_STATE:WIDTH + (g + 1) * D_STATE].astype(BF16)
            c_g = xbc[:, WIDTH + (2 + g) * D_STATE:WIDTH + (3 + g) * D_STATE].astype(BF16)
            cb = _dot_nt(c_g, b_g)
            for j in range(2):
                pair = g * 2 + j
                h0, h1 = 2 * pair, 2 * pair + 1
                cols = slice(pair * 128, (pair + 1) * 128)
                xp = xbc[:, cols]
                xd = xp * jnp.where(left, dt[:, h0:h0 + 1], dt[:, h1:h1 + 1])
                xd_b = xd.astype(BF16)
                ys = []
                for h in (h0, h1):
                    seg = cs[:, h:h + 1] - cs_t[h:h + 1, :]
                    m = jnp.exp(jnp.where(causal, seg, -jnp.inf)) * cb
                    ys.append(jnp.dot(m.astype(BF16), xd_b, preferred_element_type=F32))
                y_diag = jnp.where(left, ys[0], ys[1])
                hp = h_ref[pair]
                y_off = _dot_nt(c_g, hp.astype(BF16))
                y_off = y_off * jnp.where(left, e_cs[:, h0:h0 + 1], e_cs[:, h1:h1 + 1])
                xdd = xd * jnp.where(left, e_end[:, h0:h0 + 1], e_end[:, h1:h1 + 1])
                dec = jnp.where(top, e_last[:, h0:h0 + 1], e_last[:, h1:h1 + 1])
                h_ref[pair] = hp * dec + _dot_tn(xdd.astype(BF16), b_g)
                y = y_diag + y_off + dsk_ref[:, cols] * xp
                y_ref[:, cols] = y * _silu(dz_ref[rows, cols])
        o_ref[rows, :] = _rmsnorm(y_ref[...], ng_ref[...]).astype(o_ref.dtype)


def _mixer_d(z, conv_w, conv_b, dt_bias, a_log, d_skip, norm_g, batch, seq):
    n = z.shape[0]
    tm = min(TM_D, seq)
    nt = seq // tm
    return pl.pallas_call(
        _mixd_body,
        grid=(batch, nt),
        in_specs=[pl.BlockSpec((tm, WIDTH), lambda b, t: (b * nt + t, Z_DZ // WIDTH)),
                  pl.BlockSpec((tm, 2 * WIDTH), lambda b, t: (b * nt + t, Z_DXBC // (2 * WIDTH))),
                  pl.BlockSpec((tm, 128), lambda b, t: (b * nt + t, Z_DT // 128)),
                  pl.BlockSpec((4, 2 * WIDTH), lambda b, t: (0, 0)),
                  pl.BlockSpec((1, 2 * WIDTH), lambda b, t: (0, 0)),
                  pl.BlockSpec((1, 128), lambda b, t: (0, 0)),
                  pl.BlockSpec((1, 128), lambda b, t: (0, 0)),
                  pl.BlockSpec((1, WIDTH), lambda b, t: (0, 0)),
                  pl.BlockSpec((1, WIDTH), lambda b, t: (0, 0))],
        out_specs=pl.BlockSpec((tm, WIDTH), lambda b, t: (b * nt + t, 0)),
        out_shape=jax.ShapeDtypeStruct((n, WIDTH), BF16),
        scratch_shapes=[pltpu.VMEM((tm + D_PAD, 2 * WIDTH), F32),
                        pltpu.VMEM((N_HEADS // 2, 2 * HEAD, D_STATE), F32),
                        pltpu.VMEM((D_CHUNK, WIDTH), F32)],
        compiler_params=_cp("parallel", "arbitrary"),
        name="mixer_d",
    )(z, z, z, conv_w, conv_b, dt_bias, a_log, d_skip, norm_g)


C_PAD = 8


def _head_sum(x, ones_blk):
    hi = x.astype(BF16)
    lo = (x - hi.astype(F32)).astype(BF16)
    return (jnp.dot(hi, ones_blk, preferred_element_type=F32)
            + jnp.dot(lo, ones_blk, preferred_element_type=F32))


def _mixc_body(r_ref, k_ref, v_ref, lo_ref, mr_ref, mk_ref, mv_ref, ml_ref,
               w0_ref, w2_ref, a0_ref, a2_ref, g2_ref, kk_ref, ka_ref, rk_ref, lng_ref, lnb_ref,
               o_ref,
               rpad, kpad, vpad, lpad, st_ref, r_s, k_s, v_s, a_s, b_s, d_s, g_s, o_s):
    t = pl.program_id(1)
    tm = r_ref.shape[0]
    C = C_CHUNK

    pads = ((r_ref, rpad), (k_ref, kpad), (v_ref, vpad), (lo_ref, lpad))

    @pl.when(t == 0)
    def _():
        for _, pad in pads:
            pad[0:C_PAD, :] = jnp.zeros((C_PAD, pad.shape[1]), F32)
        st_ref[...] = jnp.zeros(st_ref.shape, F32)

    @pl.when(t > 0)
    def _():
        for _, pad in pads:
            pad[0:C_PAD, :] = pad[tm:tm + C_PAD, :]

    for src, pad in pads:
        pad[C_PAD:C_PAD + tm, :] = src[...]

    def shifted(src, pad, mix_ref):
        x = src[...]
        return x + (pad[C_PAD - 1:C_PAD - 1 + tm, :] - x) * mix_ref[...]

    hrow = lax.broadcasted_iota(jnp.int32, (WIDTH, WIDTH), 0) // HEAD
    hcol = lax.broadcasted_iota(jnp.int32, (WIDTH, WIDTH), 1) // HEAD
    ones_blk = jnp.where(hrow == hcol, 1.0, 0.0).astype(BF16)

    r = shifted(r_ref, rpad, mr_ref)
    k = shifted(k_ref, kpad, mk_ref)
    v = shifted(v_ref, vpad, mv_ref)
    lo = shifted(lo_ref, lpad, ml_ref)
    w_log = -_softplus(-(w0_ref[...] + _bdot(jnp.tanh(lo), w2_ref[...]))) - 0.5
    a = _sigmoid(a0_ref[...] + _bdot(lo, a2_ref[...]))
    g_s[...] = _bdot(_sigmoid(lo), g2_ref[...])
    kk = k * kk_ref[...]
    kk = kk / jnp.maximum(jnp.sqrt(_head_sum(kk * kk, ones_blk)), 1e-12)
    k2 = k * (1.0 + (a - 1.0) * ka_ref[...])
    r_s[...] = r
    k_s[...] = k2
    v_s[...] = v
    a_s[...] = kk
    b_s[...] = kk * a
    d_s[...] = -jnp.exp(w_log)

    row = lax.broadcasted_iota(jnp.int32, (C, C), 0)
    col = lax.broadcasted_iota(jnp.int32, (C, C), 1)
    strict = row > col
    incl = row >= col
    tril = jnp.where(incl, 1.0, 0.0).astype(F32)
    eye = jnp.where(row == col, 1.0, 0.0).astype(F32)
    cdot = functools.partial(jnp.dot, preferred_element_type=F32, precision=HIGHEST)

    def chunk(c, carry):
        rows = pl.ds(pl.multiple_of(c * C, C), C)
        ld = d_s[rows, :]
        cs = _fdot(tril, ld)
        p_in = jnp.exp(cs)
        p_inv = jnp.exp(-cs)
        rt = r_s[rows, :] * p_in
        kt = k_s[rows, :] * p_inv
        bt = b_s[rows, :] * p_inv
        at = a_s[rows, :] * jnp.exp(cs - ld)
        vv = v_s[rows, :]
        p_end = p_in[C - 1:C, :]
        for h in range(N_HEADS):
            sl = slice(h * HEAD, (h + 1) * HEAD)
            a_h, r_h, k_h, b_h, v_h = at[:, sl], rt[:, sl], kt[:, sl], bt[:, sl], vv[:, sl]
            ar = jnp.concatenate([a_h, r_h], axis=0)
            m_b = _dot_nt(ar, b_h, HIGHEST)
            m_k = _dot_nt(ar, k_h, HIGHEST)
            x = -jnp.where(strict, m_b[:C], 0.0)
            inv = eye + x
            xp = x
            for _ in range(5):
                xp = cdot(xp, xp)
                inv = inv + cdot(inv, xp)
            s = st_ref[h]
            rhs = _dot_nt(a_h, s, HIGHEST) + cdot(jnp.where(strict, m_k[:C], 0.0), v_h)
            u = -cdot(inv, rhs)
            o = (_dot_nt(r_h, s, HIGHEST) + cdot(jnp.where(incl, m_b[C:], 0.0), u)
                 + cdot(jnp.where(incl, m_k[C:], 0.0), v_h))
            st_ref[h] = (s + _dot_tn(u, b_h, HIGHEST) + _dot_tn(v_h, k_h, HIGHEST)) * p_end[:, sl]
            o_s[rows, sl] = o
        return carry

    lax.fori_loop(0, tm // C, chunk, 0)

    o = o_s[...]
    mu = _head_sum(o, ones_blk) * (1.0 / HEAD)
    d = o - mu
    var = _head_sum(d * d, ones_blk) * (1.0 / HEAD)
    on = d * lax.rsqrt(var + GN_EPS) * lng_ref[...] + lnb_ref[...]
    bonus = _head_sum(r_s[...] * k_s[...] * rk_ref[...], ones_blk) * v_s[...]
    o_ref[...] = ((on + bonus) * g_s[...]).astype(o_ref.dtype)


def _mixer_c(z, mix, w0, w2p, a0, a2p, g2p, k_k, k_a, r_k, ln_g, ln_b, batch, seq):
    n = z.shape[0]
    tm = min(TM_C, seq)
    nt = seq // tm
    lw = 256
    tok = lambda off, w: pl.BlockSpec((tm, w), lambda b, t: (b * nt + t, off // w))
    par = lambda rows, w: pl.BlockSpec((rows, w), lambda b, t: (0, 0))
    vm = lambda w: pltpu.VMEM((tm, w), F32)
    return pl.pallas_call(
        _mixc_body,
        grid=(batch, nt),
        in_specs=[tok(Z_R, WIDTH), tok(Z_K, WIDTH), tok(Z_V, WIDTH), tok(Z_LORA, lw),
                  par(1, WIDTH), par(1, WIDTH), par(1, WIDTH), par(1, lw),
                  par(1, WIDTH), par(lw, WIDTH), par(1, WIDTH), par(lw, WIDTH), par(lw, WIDTH),
                  par(1, WIDTH), par(1, WIDTH), par(1, WIDTH), par(1, WIDTH), par(1, WIDTH)],
        out_specs=pl.BlockSpec((tm, WIDTH), lambda b, t: (b * nt + t, 0)),
        out_shape=jax.ShapeDtypeStruct((n, WIDTH), BF16),
        scratch_shapes=[pltpu.VMEM((tm + C_PAD, WIDTH), F32), pltpu.VMEM((tm + C_PAD, WIDTH), F32),
                        pltpu.VMEM((tm + C_PAD, WIDTH), F32), pltpu.VMEM((tm + C_PAD, lw), F32),
                        pltpu.VMEM((N_HEADS, HEAD, HEAD), F32),
                        vm(WIDTH), vm(WIDTH), vm(WIDTH), vm(WIDTH), vm(WIDTH), vm(WIDTH), vm(WIDTH),
                        vm(WIDTH)],
        compiler_params=_cp("parallel", "arbitrary"),
        name="mixer_c",
    )(z, z, z, z, mix[:, 0:WIDTH], mix[:, WIDTH:2 * WIDTH], mix[:, 2 * WIDTH:3 * WIDTH],
      mix[:, 3 * WIDTH:], w0, w2p, a0, a2p, g2p, k_k, k_a, r_k, ln_g, ln_b)


def _merge_body(ya_ref, yb_ref, yc_ref, yd_ref, zg_ref, x_ref, wb_ref, wo_ref, o_ref):
    merged = None
    for kk, y_ref in enumerate((ya_ref, yb_ref, yc_ref, yd_ref)):
        proj = jnp.dot(y_ref[...], wb_ref[kk], preferred_element_type=F32)
        gate = _sigmoid(zg_ref[:, kk * D_MODEL:(kk + 1) * D_MODEL])
        merged = gate * proj if merged is None else merged + gate * proj
    o_ref[...] = x_ref[...] + jnp.dot(merged.astype(BF16), wo_ref[...], preferred_element_type=F32)


def _merge(ys, z, x2, wb_all, wo_all, layer):
    n = x2.shape[0]
    tm = min(TM_MERGE, n)
    yspec = pl.BlockSpec((tm, WIDTH), lambda i: (i, 0))
    return pl.pallas_call(
        _merge_body,
        grid=(n // tm,),
        in_specs=[yspec, yspec, yspec, yspec,
                  pl.BlockSpec((tm, 4 * D_MODEL), lambda i: (i, Z_G // (4 * D_MODEL))),
                  pl.BlockSpec((tm, D_MODEL), lambda i: (i, 0)),
                  pl.BlockSpec((None, 4, WIDTH, D_MODEL), lambda i: (layer, 0, 0, 0)),
                  pl.BlockSpec((None, D_MODEL, D_MODEL), lambda i: (layer, 0, 0))],
        out_specs=pl.BlockSpec((tm, D_MODEL), lambda i: (i, 0)),
        out_shape=jax.ShapeDtypeStruct((n, D_MODEL), F32),
        compiler_params=_cp("parallel"),
        name="merge",
    )(*ys, z, x2, wb_all, wo_all)


def _ffn_body(te_ref, na_ref, x_ref, g_ref, wg_ref, wu_ref, wd_ref, o_ref, h_ref, acc_ref, *,
              residual):
    i = pl.program_id(0)
    j = pl.program_id(1)
    active = i < na_ref[0]

    @pl.when(jnp.logical_and(active, j == 0))
    def _():
        h_ref[...] = _rmsnorm(x_ref[...], g_ref[...]).astype(BF16)
        acc_ref[...] = jnp.zeros(acc_ref.shape, F32)

    @pl.when(active)
    def _():
        h = h_ref[...]
        gate = jnp.dot(h, wg_ref[...], preferred_element_type=F32)
        up = jnp.dot(h, wu_ref[...], preferred_element_type=F32)
        act = (_silu(gate) * up).astype(BF16)
        acc_ref[...] += jnp.dot(act, wd_ref[...], preferred_element_type=F32)

    @pl.when(j == pl.num_programs(1) - 1)
    def _():
        @pl.when(active)
        def _():
            o_ref[...] = (x_ref[...] + acc_ref[...]) if residual else acc_ref[...]

        @pl.when(jnp.logical_not(active))
        def _():
            o_ref[...] = jnp.zeros(o_ref.shape, F32)


def _ffn(rows, g, wg_all, wu_all, wd_all, layer_idx, tile_expert, n_active, tm, residual):
    n, d = rows.shape
    d_ff = wg_all.shape[-1]
    tf = TF_FFN
    grid_spec = pltpu.PrefetchScalarGridSpec(
        num_scalar_prefetch=2,
        grid=(n // tm, d_ff // tf),
        in_specs=[pl.BlockSpec((tm, d), lambda i, j, te, na: (i, 0)),
                  pl.BlockSpec((1, d), lambda i, j, te, na: (0, 0)),
                  pl.BlockSpec((None, None, d, tf), lambda i, j, te, na: (layer_idx, te[i], 0, j)),
                  pl.BlockSpec((None, None, d, tf), lambda i, j, te, na: (layer_idx, te[i], 0, j)),
                  pl.BlockSpec((None, None, tf, d), lambda i, j, te, na: (layer_idx, te[i], j, 0))],
        out_specs=pl.BlockSpec((tm, d), lambda i, j, te, na: (i, 0)),
        scratch_shapes=[pltpu.VMEM((tm, d), BF16), pltpu.VMEM((tm, d), F32)],
    )
    return pl.pallas_call(
        functools.partial(_ffn_body, residual=residual),
        grid_spec=grid_spec,
        out_shape=jax.ShapeDtypeStruct((n, d), F32),
        compiler_params=_cp("parallel", "arbitrary"),
        name="swiglu_res" if residual else "swiglu_moe",
    )(tile_expert, n_active, rows, g, wg_all, wu_all, wd_all)


def _route_body(x_ref, g_ref, wr_ref, o_ref):
    h = _rmsnorm(x_ref[...], g_ref[...])
    logits = _fdot(h, wr_ref[...])
    lane = lax.broadcasted_iota(jnp.int32, logits.shape, 1)
    lane_f = lane.astype(F32)
    neg = -jnp.inf
    logits = jnp.where(lane < N_EXPERTS, logits, neg)
    m1 = jnp.max(logits, axis=-1, keepdims=True)
    i1 = jnp.min(jnp.where(logits == m1, lane_f, 128.0), axis=-1, keepdims=True)
    rest = jnp.where(lane_f == i1, neg, logits)
    m2 = jnp.max(rest, axis=-1, keepdims=True)
    i2 = jnp.min(jnp.where(rest == m2, lane_f, 128.0), axis=-1, keepdims=True)
    e = jnp.exp(m2 - m1)
    p1 = 1.0 / (1.0 + e)
    p2 = e / (1.0 + e)
    out = jnp.where(lane == 0, i1, 0.0)
    out = jnp.where(lane == 1, i2, out)
    out = jnp.where(lane == 2, p1, out)
    out = jnp.where(lane == 3, p2, out)
    o_ref[...] = out


def _route(x2, g, wr_pad):
    n, d = x2.shape
    tm = min(TM_ROUTE, n)
    return pl.pallas_call(
        _route_body,
        grid=(n // tm,),
        in_specs=[pl.BlockSpec((tm, d), lambda i: (i, 0)),
                  pl.BlockSpec((1, d), lambda i: (0, 0)),
                  pl.BlockSpec((d, 128), lambda i: (0, 0))],
        out_specs=pl.BlockSpec((tm, 128), lambda i: (i, 0)),
        out_shape=jax.ShapeDtypeStruct((n, 128), F32),
        compiler_params=_cp("parallel"),
        name="router",
    )(x2, g, wr_pad)


def _row_copy(src_hbm, dst_vmem, src_row, dst_row, sem):
    return pltpu.make_async_copy(src_hbm.at[pl.ds(src_row, 1)], dst_vmem.at[pl.ds(dst_row, 1)], sem)


def _gather_rows(idx_ref, src_hbm, dst_vmem, sem):
    n_rows = dst_vmem.shape[0]

    def issue(r, carry):
        _row_copy(src_hbm, dst_vmem, idx_ref[0, 0, r], r, sem).start()
        return carry

    lax.fori_loop(0, n_rows, issue, 0)
    pltpu.make_async_copy(src_hbm.at[pl.ds(0, n_rows)], dst_vmem, sem).wait()


def _gather_body(idx_ref, x_hbm, o_ref, sem):
    _gather_rows(idx_ref, x_hbm, o_ref, sem)


def _gather(x2, tok_of_slot):
    n_slots = tok_of_slot.shape[0]
    d = x2.shape[1]
    tm = TM_GATHER
    idx = tok_of_slot.reshape(n_slots // tm, 1, tm)
    return pl.pallas_call(
        _gather_body,
        grid=(n_slots // tm,),
        in_specs=[pl.BlockSpec((1, 1, tm), lambda i: (i, 0, 0), memory_space=pltpu.SMEM),
                  pl.BlockSpec(memory_space=pl.ANY)],
        out_specs=pl.BlockSpec((tm, d), lambda i: (i, 0)),
        out_shape=jax.ShapeDtypeStruct((n_slots, d), F32),
        scratch_shapes=[pltpu.SemaphoreType.DMA(())],
        compiler_params=_cp("arbitrary"),
        name="moe_gather",
    )(idx, x2)


def _combine_body(s1_ref, s2_ref, y_hbm, x_ref, info_ref, g_ref, o_ref, y1_ref, y2_ref, sem1, sem2, *,
                  final_norm):
    _gather_rows(s1_ref, y_hbm, y1_ref, sem1)
    _gather_rows(s2_ref, y_hbm, y2_ref, sem2)
    out = x_ref[...] + info_ref[:, 2:3] * y1_ref[...] + info_ref[:, 3:4] * y2_ref[...]
    if final_norm:
        out = _rmsnorm(out, g_ref[...])
    o_ref[...] = out


def _combine(y_sorted, x2, info, slot1, slot2, g_final, final_norm):
    n, d = x2.shape
    tm = min(TM_COMBINE, n)
    sspec = pl.BlockSpec((1, 1, tm), lambda i: (i, 0, 0), memory_space=pltpu.SMEM)
    return pl.pallas_call(
        functools.partial(_combine_body, final_norm=final_norm),
        grid=(n // tm,),
        in_specs=[sspec, sspec,
                  pl.BlockSpec(memory_space=pl.ANY),
                  pl.BlockSpec((tm, d), lambda i: (i, 0)),
                  pl.BlockSpec((tm, 128), lambda i: (i, 0)),
                  pl.BlockSpec((1, d), lambda i: (0, 0))],
        out_specs=pl.BlockSpec((tm, d), lambda i: (i, 0)),
        out_shape=jax.ShapeDtypeStruct((n, d), F32),
        scratch_shapes=[pltpu.VMEM((tm, d), F32), pltpu.VMEM((tm, d), F32),
                        pltpu.SemaphoreType.DMA(()), pltpu.SemaphoreType.DMA(())],
        compiler_params=_cp("arbitrary"),
        name="moe_combine",
    )(slot1.reshape(n // tm, 1, tm), slot2.reshape(n // tm, 1, tm), y_sorted, x2, info, g_final)


def _dispatch_plan(info, tm):
    n = info.shape[0]
    experts = info[:, 0:2].astype(jnp.int32)
    flat = experts.reshape(-1)
    onehot = (flat[:, None] == jnp.arange(N_EXPERTS, dtype=jnp.int32)[None, :]).astype(jnp.int32)
    csum = jnp.cumsum(onehot, axis=0)
    rank = jnp.sum(csum * onehot, axis=1) - 1
    counts = csum[-1]
    padded = ((counts + tm - 1) // tm) * tm
    ends = jnp.cumsum(padded)
    starts = ends - padded
    slot = (jnp.sum(starts[None, :] * onehot, axis=1) + rank).astype(jnp.int32)
    n_slots = 2 * n + N_EXPERTS * tm
    token = jnp.arange(2 * n, dtype=jnp.int32) // 2
    tok_of_slot = jnp.zeros((n_slots,), jnp.int32).at[slot].set(token)
    tile_start = jnp.arange(n_slots // tm, dtype=jnp.int32) * tm
    tile_expert = jnp.minimum(
        jnp.sum((tile_start[:, None] >= ends[None, :]).astype(jnp.int32), axis=1), N_EXPERTS - 1)
    n_active = (ends[-1] // tm).astype(jnp.int32).reshape(1)
    slots = slot.reshape(n, 2)
    return tok_of_slot, tile_expert.astype(jnp.int32), n_active, slots[:, 0], slots[:, 1]


def _prep_w_in(w_in):
    sl = lambda a, b: w_in[:, :, a:b]
    parts = [sl(4872, 8968), sl(0, 1024), sl(3840, 4864), sl(1024, 1536), sl(1536, 3072),
             sl(3328, 3840), sl(3072, 3328), sl(4864, 4872)]
    w = jnp.concatenate(parts, axis=-1)
    w = jnp.pad(w, ((0, 0), (0, 0), (0, Z_WIDTH - w.shape[-1])))
    return w.astype(BF16)


def kernel(x, norm_mix_g, w_in, a_ln_g, a_ln_b, a_w_s, a_b_s, b_w_pool, b_scale, c_mix, c_w0, c_w2, c_a0, c_a2, c_g2, c_k_k, c_k_a, c_r_k, c_ln_g, c_ln_b, d_conv_w, d_conv_b, d_dt_bias, d_a_log, d_skip, d_norm_g, w_branch, w_out, norm_ffn_g, ffn_w_gate, ffn_w_up, ffn_w_down, moe_router, moe_w_gate, moe_w_up, moe_w_down, norm_final_g):
    batch, seq, d = x.shape
    depth = w_in.shape[0]
    assert depth % 2 == 0, "the final rmsnorm is fused into the last routed layer's combine"
    n = batch * seq
    x2 = x.reshape(n, d)

    w_in_p = _prep_w_in(w_in)
    wb_all = w_branch.astype(BF16)
    wo_all = w_out.astype(BF16)
    ffn_wg = ffn_w_gate.astype(BF16)[:, None]
    ffn_wu = ffn_w_up.astype(BF16)[:, None]
    ffn_wd = ffn_w_down.astype(BF16)[:, None]
    moe_wg = moe_w_gate.astype(BF16)
    moe_wu = moe_w_up.astype(BF16)
    moe_wd = moe_w_down.astype(BF16)

    row = lambda v: v.reshape(1, -1)
    pad_lanes = lambda v: jnp.pad(v, (0, 128 - v.shape[0])).reshape(1, 128)
    dense_tiles = n // min(TM_FFN, n)
    dense_te = jnp.zeros((dense_tiles,), jnp.int32)
    dense_na = jnp.full((1,), dense_tiles, jnp.int32)

    for l in range(depth):
        z = _in_proj(x2, row(norm_mix_g[l]), w_in_p, l)
        y_a = _mixer_a(z, row(a_ln_g[l]), row(a_ln_b[l]), a_w_s[l], a_b_s[l].T, seq)
        y_b = _mixer_b(z, b_w_pool[l], row(b_scale[l]), batch, seq)
        w2p = jnp.pad(c_w2[l], ((0, 192), (0, 0)))
        a2p = jnp.pad(c_a2[l], ((64, 128), (0, 0)))
        g2p = jnp.pad(c_g2[l], ((128, 0), (0, 0)))
        y_c = _mixer_c(z, row(c_mix[l]), row(c_w0[l]), w2p, row(c_a0[l]), a2p, g2p,
                       row(c_k_k[l]), row(c_k_a[l]), row(c_r_k[l]), row(c_ln_g[l]), row(c_ln_b[l]),
                       batch, seq)
        y_d = _mixer_d(z, d_conv_w[l], row(d_conv_b[l]), pad_lanes(d_dt_bias[l]),
                       pad_lanes(d_a_log[l]), row(jnp.repeat(d_skip[l], HEAD)), row(d_norm_g[l]),
                       batch, seq)
        x2 = _merge((y_a, y_b, y_c, y_d), z, x2, wb_all, wo_all, l)

        g_ffn = row(norm_ffn_g[l])
        if l % 2 == 0:
            x2 = _ffn(x2, g_ffn, ffn_wg, ffn_wu, ffn_wd, l // 2, dense_te, dense_na,
                      min(TM_FFN, n), True)
        else:
            i = l // 2
            wr_pad = jnp.pad(moe_router[i], ((0, 0), (0, 128 - N_EXPERTS)))
            info = _route(x2, g_ffn, wr_pad)
            tok_of_slot, tile_expert, n_active, slot1, slot2 = _dispatch_plan(info, TM_MOE)
            rows = _gather(x2, tok_of_slot)
            y_sorted = _ffn(rows, g_ffn, moe_wg, moe_wu, moe_wd, i, tile_expert, n_active,
                            TM_MOE, False)
            x2 = _combine(y_sorted, x2, info, slot1, slot2, row(norm_final_g), l == depth - 1)
    return x2.reshape(batch, seq, d)
```

```python
import functools

import jax
import jax.numpy as jnp
from jax import lax
from jax.experimental import pallas as pl
from jax.experimental.pallas import tpu as pltpu

F32 = jnp.float32
BF16 = jnp.bfloat16
HIGHEST = lax.Precision.HIGHEST

EPS = 1e-6
GN_EPS = 64e-5
D_MODEL = 1024
N_HEADS = 8
HEAD = 64
WIDTH = 512
A_CHUNK = 128
D_CHUNK = 128
D_STATE = 128
C_CHUNK = 64
B_WINDOWS = (2, 4, 8, 16)
N_EXPERTS = 8
D_FF = 3584

Z_G, Z_A, Z_DXBC, Z_B, Z_R, Z_K, Z_V, Z_DZ, Z_LORA, Z_DT = (
    0, 4096, 5120, 6144, 6656, 7168, 7680, 8192, 8704, 8960)
Z_WIDTH = 9216

TM_PROJ = 1024
TN_PROJ = 1024
TM_A = 512
TM_B = 512
TM_C = 256
TM_D = 128
TM_MERGE = 256
TM_FFN = 1024
TM_MOE = 512
TF_FFN = 512
TM_ROUTE = 512
TM_GATHER = 256
TM_COMBINE = 256
VMEM_LIMIT = 48 * 1024 * 1024


def _cp(*sem):
    return pltpu.CompilerParams(dimension_semantics=sem, vmem_limit_bytes=VMEM_LIMIT)


def _bdot(a, b):
    return jnp.dot(a.astype(BF16), b.astype(BF16), preferred_element_type=F32)


def _fdot(a, b):
    return jnp.dot(a, b, preferred_element_type=F32, precision=HIGHEST)


def _dot_nt(a, b, precision=None):
    return lax.dot_general(a, b, (((1,), (1,)), ((), ())), preferred_element_type=F32,
                           precision=precision)


def _dot_tn(a, b, precision=None):
    return lax.dot_general(a, b, (((0,), (0,)), ((), ())), preferred_element_type=F32,
                           precision=precision)


def _sigmoid(x):
    return 1.0 / (1.0 + jnp.exp(-x))


def _silu(x):
    return x * _sigmoid(x)


def _softplus(x):
    return jnp.maximum(x, 0.0) + jnp.log(1.0 + jnp.exp(-jnp.abs(x)))


def _rmsnorm(x, g):
    return x * lax.rsqrt(jnp.mean(x * x, axis=-1, keepdims=True) + EPS) * g


def _inproj_body(x_ref, g_ref, w_ref, o_ref, h_ref):
    @pl.when(pl.program_id(1) == 0)
    def _():
        h_ref[...] = _rmsnorm(x_ref[...], g_ref[...]).astype(BF16)

    o_ref[...] = jnp.dot(h_ref[...], w_ref[...], preferred_element_type=F32).astype(o_ref.dtype)


def _in_proj(x2, g, w_all, layer):
    n, d = x2.shape
    zw = w_all.shape[-1]
    tm = min(TM_PROJ, n)
    tn = TN_PROJ
    return pl.pallas_call(
        _inproj_body,
        grid=(n // tm, zw // tn),
        in_specs=[pl.BlockSpec((tm, d), lambda i, j: (i, 0)),
                  pl.BlockSpec((1, d), lambda i, j: (0, 0)),
                  pl.BlockSpec((None, d, tn), lambda i, j: (layer, 0, j))],
        out_specs=pl.BlockSpec((tm, tn), lambda i, j: (i, j)),
        out_shape=jax.ShapeDtypeStruct((n, zw), BF16),
        scratch_shapes=[pltpu.VMEM((tm, d), BF16)],
        compiler_params=_cp("parallel", "arbitrary"),
        name="in_proj",
    )(x2, g, w_all)


def _mixa_body(uv_ref, lng_ref, lnb_ref, ws_ref, bs_ref, o_ref):
    tm = uv_ref.shape[0]
    row = lax.broadcasted_iota(jnp.int32, (A_CHUNK, A_CHUNK), 0)
    col = lax.broadcasted_iota(jnp.int32, (A_CHUNK, A_CHUNK), 1)
    ws = [jnp.where(row >= col, ws_ref[g], 0.0).astype(BF16) for g in range(4)]
    for c in range(tm // A_CHUNK):
        rows = slice(c * A_CHUNK, (c + 1) * A_CHUNK)
        uv = jax.nn.gelu(uv_ref[rows, :].astype(F32))
        u = uv[:, :WIDTH]
        v = uv[:, WIDTH:]
        mu = jnp.mean(v, axis=-1, keepdims=True)
        d = v - mu
        var = jnp.mean(d * d, axis=-1, keepdims=True)
        vn = d * lax.rsqrt(var + EPS) * lng_ref[...] + lnb_ref[...]
        for g in range(4):
            cols = slice(g * 128, (g + 1) * 128)
            sg = jnp.dot(ws[g], vn[:, cols].astype(BF16), preferred_element_type=F32)
            sg = sg + bs_ref[:, g:g + 1]
            o_ref[rows, cols] = (u[:, cols] * sg).astype(o_ref.dtype)


def _mixer_a(z, ln_g, ln_b, w_s, b_s_t, seq):
    n = z.shape[0]
    tm = min(TM_A, seq)
    return pl.pallas_call(
        _mixa_body,
        grid=(n // tm,),
        in_specs=[pl.BlockSpec((tm, 2 * WIDTH), lambda i: (i, Z_A // (2 * WIDTH))),
                  pl.BlockSpec((1, WIDTH), lambda i: (0, 0)),
                  pl.BlockSpec((1, WIDTH), lambda i: (0, 0)),
                  pl.BlockSpec((4, A_CHUNK, A_CHUNK), lambda i: (0, 0, 0)),
                  pl.BlockSpec((A_CHUNK, 4), lambda i: (0, 0))],
        out_specs=pl.BlockSpec((tm, WIDTH), lambda i: (i, 0)),
        out_shape=jax.ShapeDtypeStruct((n, WIDTH), BF16),
        compiler_params=_cp("parallel"),
        name="mixer_a",
    )(z, ln_g, ln_b, w_s, b_s_t)


B_PAD = 16


def _mixb_body(p_ref, w_ref, sc_ref, o_ref, pad_ref):
    t = pl.program_id(1)
    tm = p_ref.shape[0]

    @pl.when(t == 0)
    def _():
        pad_ref[0:B_PAD, :] = jnp.zeros((B_PAD, WIDTH), F32)

    @pl.when(t > 0)
    def _():
        pad_ref[0:B_PAD, :] = pad_ref[tm:tm + B_PAD, :]

    pad_ref[B_PAD:B_PAD + tm, :] = p_ref[...].astype(F32)
    pos = (t * tm + lax.broadcasted_iota(jnp.int32, (tm, 1), 0)).astype(F32)
    for g, win in enumerate(B_WINDOWS):
        cols = slice(g * 128, (g + 1) * 128)
        p_g = pad_ref[B_PAD:B_PAD + tm, cols]
        acc = p_g
        for j in range(1, win):
            acc = acc + pad_ref[B_PAD - j:B_PAD - j + tm, cols]
        pooled = acc / jnp.minimum(pos + 1.0, float(win)) - p_g
        y = _bdot(pooled, w_ref[g]) * sc_ref[:, cols]
        o_ref[:, cols] = y.astype(o_ref.dtype)


def _mixer_b(z, w_pool, scale, batch, seq):
    n = z.shape[0]
    tm = min(TM_B, seq)
    nt = seq // tm
    return pl.pallas_call(
        _mixb_body,
        grid=(batch, nt),
        in_specs=[pl.BlockSpec((tm, WIDTH), lambda b, t: (b * nt + t, Z_B // WIDTH)),
                  pl.BlockSpec((4, 128, 128), lambda b, t: (0, 0, 0)),
                  pl.BlockSpec((1, WIDTH), lambda b, t: (0, 0))],
        out_specs=pl.BlockSpec((tm, WIDTH), lambda b, t: (b * nt + t, 0)),
        out_shape=jax.ShapeDtypeStruct((n, WIDTH), BF16),
        scratch_shapes=[pltpu.VMEM((tm + B_PAD, WIDTH), F32)],
        compiler_params=_cp("parallel", "arbitrary"),
        name="mixer_b",
    )(z, w_pool, scale)


D_PAD = 8


def _mixd_body(dz_ref, xbc_ref, dt_ref, cw_ref, cb_ref, dtb_ref, alog_ref, dsk_ref, ng_ref,
               o_ref, pad_ref, h_ref, y_ref):
    t = pl.program_id(1)
    tm = xbc_ref.shape[0]
    L = D_CHUNK

    @pl.when(t == 0)
    def _():
        pad_ref[0:D_PAD, :] = jnp.zeros((D_PAD, 2 * WIDTH), F32)
        h_ref[...] = jnp.zeros(h_ref.shape, F32)

    @pl.when(t > 0)
    def _():
        pad_ref[0:D_PAD, :] = pad_ref[tm:tm + D_PAD, :]

    pad_ref[D_PAD:D_PAD + tm, :] = xbc_ref[...].astype(F32)

    row = lax.broadcasted_iota(jnp.int32, (L, L), 0)
    col = lax.broadcasted_iota(jnp.int32, (L, L), 1)
    causal = row >= col
    tril = jnp.where(causal, 1.0, 0.0).astype(F32)
    left = col < HEAD
    top = row < HEAD
    a_head = -jnp.exp(alog_ref[...])

    for c in range(tm // L):
        base = D_PAD + c * L
        rows = slice(c * L, (c + 1) * L)
        conv = cb_ref[...]
        for k in range(4):
            conv = conv + cw_ref[k:k + 1, :] * pad_ref[base - 3 + k:base - 3 + k + L, :]
        xbc = _silu(conv)
        dt = _softplus(dt_ref[rows, :].astype(F32) + dtb_ref[...])
        cs = _fdot(tril, dt * a_head)
        cs_t = cs.T
        last = cs[L - 1:L, :]
        e_cs = jnp.exp(cs)
        e_end = jnp.exp(last - cs)
        e_last = jnp.exp(last)
        for g in range(2):
            b_g = xbc[:, WIDTH + g * D_STATE:WIDTH + (g + 1) * D_STATE].astype(BF16)
            c_g = xbc[:, WIDTH + (2 + g) * D_STATE:WIDTH + (3 + g) * D_STATE].astype(BF16)
            cb = _dot_nt(c_g, b_g)
            for j in range(2):
                pair = g * 2 + j
                h0, h1 = 2 * pair, 2 * pair + 1
                cols = slice(pair * 128, (pair + 1) * 128)
                xp = xbc[:, cols]
                xd = xp * jnp.where(left, dt[:, h0:h0 + 1], dt[:, h1:h1 + 1])
                xd_b = xd.astype(BF16)
                ys = []
                for h in (h0, h1):
                    seg = cs[:, h:h + 1] - cs_t[h:h + 1, :]
                    m = jnp.exp(jnp.where(causal, seg, -jnp.inf)) * cb
                    ys.append(jnp.dot(m.astype(BF16), xd_b, preferred_element_type=F32))
                y_diag = jnp.where(left, ys[0], ys[1])
                hp = h_ref[pair]
                y_off = _dot_nt(c_g, hp.astype(BF16))
                y_off = y_off * jnp.where(left, e_cs[:, h0:h0 + 1], e_cs[:, h1:h1 + 1])
                xdd = xd * jnp.where(left, e_end[:, h0:h0 + 1], e_end[:, h1:h1 + 1])
                dec = jnp.where(top, e_last[:, h0:h0 + 1], e_last[:, h1:h1 + 1])
                h_ref[pair] = hp * dec + _dot_tn(xdd.astype(BF16), b_g)
                y = y_diag + y_off + dsk_ref[:, cols] * xp
                y_ref[:, cols] = y * _silu(dz_ref[rows, cols].astype(F32))
        o_ref[rows, :] = _rmsnorm(y_ref[...], ng_ref[...]).astype(o_ref.dtype)


def _mixer_d(z, conv_w, conv_b, dt_bias, a_log, d_skip, norm_g, batch, seq):
    n = z.shape[0]
    tm = min(TM_D, seq)
    nt = seq // tm
    return pl.pallas_call(
        _mixd_body,
        grid=(batch, nt),
        in_specs=[pl.BlockSpec((tm, WIDTH), lambda b, t: (b * nt + t, Z_DZ // WIDTH)),
                  pl.BlockSpec((tm, 2 * WIDTH), lambda b, t: (b * nt + t, Z_DXBC // (2 * WIDTH))),
                  pl.BlockSpec((tm, 128), lambda b, t: (b * nt + t, Z_DT // 128)),
                  pl.BlockSpec((4, 2 * WIDTH), lambda b, t: (0, 0)),
                  pl.BlockSpec((1, 2 * WIDTH), lambda b, t: (0, 0)),
                  pl.BlockSpec((1, 128), lambda b, t: (0, 0)),
                  pl.BlockSpec((1, 128), lambda b, t: (0, 0)),
                  pl.BlockSpec((1, WIDTH), lambda b, t: (0, 0)),
                  pl.BlockSpec((1, WIDTH), lambda b, t: (0, 0))],
        out_specs=pl.BlockSpec((tm, WIDTH), lambda b, t: (b * nt + t, 0)),
        out_shape=jax.ShapeDtypeStruct((n, WIDTH), BF16),
        scratch_shapes=[pltpu.VMEM((tm + D_PAD, 2 * WIDTH), F32),
                        pltpu.VMEM((N_HEADS // 2, 2 * HEAD, D_STATE), F32),
                        pltpu.VMEM((D_CHUNK, WIDTH), F32)],
        compiler_params=_cp("parallel", "arbitrary"),
        name="mixer_d",
    )(z, z, z, conv_w, conv_b, dt_bias, a_log, d_skip, norm_g)


C_PAD = 8


def _head_sum(x, ones_blk):
    hi = x.astype(BF16)
    lo = (x - hi.astype(F32)).astype(BF16)
    return (jnp.dot(hi, ones_blk, preferred_element_type=F32)
            + jnp.dot(lo, ones_blk, preferred_element_type=F32))


def _mixc_body(r_ref, k_ref, v_ref, lo_ref, mr_ref, mk_ref, mv_ref, ml_ref,
               w0_ref, w2_ref, a0_ref, a2_ref, g2_ref, kk_ref, ka_ref, rk_ref, lng_ref, lnb_ref,
               o_ref,
               rpad, kpad, vpad, lpad, st_ref, r_s, k_s, v_s, a_s, b_s, d_s, g_s, o_s):
    t = pl.program_id(1)
    tm = r_ref.shape[0]
    C = C_CHUNK

    pads = ((r_ref, rpad), (k_ref, kpad), (v_ref, vpad), (lo_ref, lpad))

    @pl.when(t == 0)
    def _():
        for _, pad in pads:
            pad[0:C_PAD, :] = jnp.zeros((C_PAD, pad.shape[1]), F32)
        st_ref[...] = jnp.zeros(st_ref.shape, F32)

    @pl.when(t > 0)
    def _():
        for _, pad in pads:
            pad[0:C_PAD, :] = pad[tm:tm + C_PAD, :]

    for src, pad in pads:
        pad[C_PAD:C_PAD + tm, :] = src[...].astype(F32)

    def shifted(src, pad, mix_ref):
        x = pad[C_PAD:C_PAD + tm, :]
        return x + (pad[C_PAD - 1:C_PAD - 1 + tm, :] - x) * mix_ref[...]

    hrow = lax.broadcasted_iota(jnp.int32, (WIDTH, WIDTH), 0) // HEAD
    hcol = lax.broadcasted_iota(jnp.int32, (WIDTH, WIDTH), 1) // HEAD
    ones_blk = jnp.where(hrow == hcol, 1.0, 0.0).astype(BF16)

    r = shifted(r_ref, rpad, mr_ref)
    k = shifted(k_ref, kpad, mk_ref)
    v = shifted(v_ref, vpad, mv_ref)
    lo = shifted(lo_ref, lpad, ml_ref)
    w_log = -_softplus(-(w0_ref[...] + _bdot(jnp.tanh(lo), w2_ref[...]))) - 0.5
    a = _sigmoid(a0_ref[...] + _bdot(lo, a2_ref[...]))
    g_s[...] = _bdot(_sigmoid(lo), g2_ref[...])
    kk = k * kk_ref[...]
    kk = kk / jnp.maximum(jnp.sqrt(_head_sum(kk * kk, ones_blk)), 1e-12)
    k2 = k * (1.0 + (a - 1.0) * ka_ref[...])
    r_s[...] = r
    k_s[...] = k2
    v_s[...] = v
    a_s[...] = kk
    b_s[...] = kk * a
    d_s[...] = -jnp.exp(w_log)

    row = lax.broadcasted_iota(jnp.int32, (C, C), 0)
    col = lax.broadcasted_iota(jnp.int32, (C, C), 1)
    strict = row > col
    incl = row >= col
    tril = jnp.where(incl, 1.0, 0.0).astype(F32)
    eye = jnp.where(row == col, 1.0, 0.0).astype(F32)

    def chunk(c, carry):
        rows = pl.ds(pl.multiple_of(c * C, C), C)
        ld = d_s[rows, :]
        cs = _fdot(tril, ld)
        p_in = jnp.exp(cs)
        p_inv = jnp.exp(-cs)
        rt = r_s[rows, :] * p_in
        kt = k_s[rows, :] * p_inv
        bt = b_s[rows, :] * p_inv
        at = a_s[rows, :] * jnp.exp(cs - ld)
        vv = v_s[rows, :]
        p_end = p_in[C - 1:C, :]
        heads = range(N_HEADS)
        sls = [slice(h * HEAD, (h + 1) * HEAD) for h in heads]
        ar = [jnp.concatenate([at[:, sl], rt[:, sl]], axis=0).astype(BF16) for sl in sls]
        b_b = [bt[:, sl].astype(BF16) for sl in sls]
        k_b = [kt[:, sl].astype(BF16) for sl in sls]
        v_b = [vv[:, sl].astype(BF16) for sl in sls]
        s = [st_ref[h] for h in heads]
        m_b = [_dot_nt(ar[h], b_b[h]) for h in heads]
        m_k = [_dot_nt(ar[h], k_b[h]) for h in heads]
        m_s = [_dot_nt(ar[h], s[h].astype(BF16)) for h in heads]
        xp = [(-jnp.where(strict, m_b[h][:C], 0.0)).astype(BF16) for h in heads]
        inv = [eye + xp[h].astype(F32) for h in heads]
        rhs = [m_s[h][:C] + _bdot(jnp.where(strict, m_k[h][:C], 0.0), v_b[h]) for h in heads]
        for _ in range(5):
            xp = [_bdot(xp[h], xp[h]).astype(BF16) for h in heads]
            inv = [inv[h] + _bdot(inv[h], xp[h]) for h in heads]
        u_b = [(-_bdot(inv[h], rhs[h])).astype(BF16) for h in heads]
        for h in heads:
            o = (m_s[h][C:] + _bdot(jnp.where(incl, m_b[h][C:], 0.0), u_b[h])
                 + _bdot(jnp.where(incl, m_k[h][C:], 0.0), v_b[h]))
            o_s[rows, sls[h]] = o
            st_ref[h] = (s[h] + _dot_tn(u_b[h], b_b[h]) + _dot_tn(v_b[h], k_b[h])) * p_end[:, sls[h]]
        return carry

    lax.fori_loop(0, tm // C, chunk, 0)

    o = o_s[...]
    mu = _head_sum(o, ones_blk) * (1.0 / HEAD)
    d = o - mu
    var = _head_sum(d * d, ones_blk) * (1.0 / HEAD)
    on = d * lax.rsqrt(var + GN_EPS) * lng_ref[...] + lnb_ref[...]
    bonus = _head_sum(r_s[...] * k_s[...] * rk_ref[...], ones_blk) * v_s[...]
    o_ref[...] = ((on + bonus) * g_s[...]).astype(o_ref.dtype)


def _mixer_c(z, mix, w0, w2p, a0, a2p, g2p, k_k, k_a, r_k, ln_g, ln_b, batch, seq):
    n = z.shape[0]
    tm = min(TM_C, seq)
    nt = seq // tm
    lw = 256
    tok = lambda off, w: pl.BlockSpec((tm, w), lambda b, t: (b * nt + t, off // w))
    par = lambda rows, w: pl.BlockSpec((rows, w), lambda b, t: (0, 0))
    vm = lambda w: pltpu.VMEM((tm, w), F32)
    return pl.pallas_call(
        _mixc_body,
        grid=(batch, nt),
        in_specs=[tok(Z_R, WIDTH), tok(Z_K, WIDTH), tok(Z_V, WIDTH), tok(Z_LORA, lw),
                  par(1, WIDTH), par(1, WIDTH), par(1, WIDTH), par(1, lw),
                  par(1, WIDTH), par(lw, WIDTH), par(1, WIDTH), par(lw, WIDTH), par(lw, WIDTH),
                  par(1, WIDTH), par(1, WIDTH), par(1, WIDTH), par(1, WIDTH), par(1, WIDTH)],
        out_specs=pl.BlockSpec((tm, WIDTH), lambda b, t: (b * nt + t, 0)),
        out_shape=jax.ShapeDtypeStruct((n, WIDTH), BF16),
        scratch_shapes=[pltpu.VMEM((tm + C_PAD, WIDTH), F32), pltpu.VMEM((tm + C_PAD, WIDTH), F32),
                        pltpu.VMEM((tm + C_PAD, WIDTH), F32), pltpu.VMEM((tm + C_PAD, lw), F32),
                        pltpu.VMEM((N_HEADS, HEAD, HEAD), F32),
                        vm(WIDTH), vm(WIDTH), vm(WIDTH), vm(WIDTH), vm(WIDTH), vm(WIDTH), vm(WIDTH),
                        vm(WIDTH)],
        compiler_params=_cp("parallel", "arbitrary"),
        name="mixer_c",
    )(z, z, z, z, mix[:, 0:WIDTH], mix[:, WIDTH:2 * WIDTH], mix[:, 2 * WIDTH:3 * WIDTH],
      mix[:, 3 * WIDTH:], w0, w2p, a0, a2p, g2p, k_k, k_a, r_k, ln_g, ln_b)


def _merge_body(ya_ref, yb_ref, yc_ref, yd_ref, zg_ref, x_ref, wb_ref, wo_ref, o_ref):
    merged = None
    for kk, y_ref in enumerate((ya_ref, yb_ref, yc_ref, yd_ref)):
        proj = jnp.dot(y_ref[...], wb_ref[kk], preferred_element_type=F32)
        gate = _sigmoid(zg_ref[:, kk * D_MODEL:(kk + 1) * D_MODEL].astype(F32))
        merged = gate * proj if merged is None else merged + gate * proj
    o_ref[...] = x_ref[...] + jnp.dot(merged.astype(BF16), wo_ref[...], preferred_element_type=F32)


def _merge(ys, z, x2, wb_all, wo_all, layer):
    n = x2.shape[0]
    tm = min(TM_MERGE, n)
    yspec = pl.BlockSpec((tm, WIDTH), lambda i: (i, 0))
    return pl.pallas_call(
        _merge_body,
        grid=(n // tm,),
        in_specs=[yspec, yspec, yspec, yspec,
                  pl.BlockSpec((tm, 4 * D_MODEL), lambda i: (i, Z_G // (4 * D_MODEL))),
                  pl.BlockSpec((tm, D_MODEL), lambda i: (i, 0)),
                  pl.BlockSpec((None, 4, WIDTH, D_MODEL), lambda i: (layer, 0, 0, 0)),
                  pl.BlockSpec((None, D_MODEL, D_MODEL), lambda i: (layer, 0, 0))],
        out_specs=pl.BlockSpec((tm, D_MODEL), lambda i: (i, 0)),
        out_shape=jax.ShapeDtypeStruct((n, D_MODEL), F32),
        compiler_params=_cp("parallel"),
        name="merge",
    )(*ys, z, x2, wb_all, wo_all)


def _ffn_body(te_ref, na_ref, x_ref, g_ref, wg_ref, wu_ref, wd_ref, o_ref, h_ref, acc_ref, *,
              residual):
    i = pl.program_id(0)
    j = pl.program_id(1)
    active = i < na_ref[0]

    @pl.when(jnp.logical_and(active, j == 0))
    def _():
        h_ref[...] = _rmsnorm(x_ref[...], g_ref[...]).astype(BF16)
        acc_ref[...] = jnp.zeros(acc_ref.shape, F32)

    @pl.when(active)
    def _():
        h = h_ref[...]
        gate = jnp.dot(h, wg_ref[...], preferred_element_type=F32)
        up = jnp.dot(h, wu_ref[...], preferred_element_type=F32)
        act = (_silu(gate) * up).astype(BF16)
        acc_ref[...] += jnp.dot(act, wd_ref[...], preferred_element_type=F32)

    @pl.when(j == pl.num_programs(1) - 1)
    def _():
        @pl.when(active)
        def _():
            o_ref[...] = (x_ref[...] + acc_ref[...]) if residual else acc_ref[...]

        @pl.when(jnp.logical_not(active))
        def _():
            o_ref[...] = jnp.zeros(o_ref.shape, F32)


def _ffn(rows, g, wg_all, wu_all, wd_all, layer_idx, tile_expert, n_active, tm, residual):
    n, d = rows.shape
    d_ff = wg_all.shape[-1]
    tf = TF_FFN
    grid_spec = pltpu.PrefetchScalarGridSpec(
        num_scalar_prefetch=2,
        grid=(n // tm, d_ff // tf),
        in_specs=[pl.BlockSpec((tm, d), lambda i, j, te, na: (i, 0)),
                  pl.BlockSpec((1, d), lambda i, j, te, na: (0, 0)),
                  pl.BlockSpec((None, None, d, tf), lambda i, j, te, na: (layer_idx, te[i], 0, j)),
                  pl.BlockSpec((None, None, d, tf), lambda i, j, te, na: (layer_idx, te[i], 0, j)),
                  pl.BlockSpec((None, None, tf, d), lambda i, j, te, na: (layer_idx, te[i], j, 0))],
        out_specs=pl.BlockSpec((tm, d), lambda i, j, te, na: (i, 0)),
        scratch_shapes=[pltpu.VMEM((tm, d), BF16), pltpu.VMEM((tm, d), F32)],
    )
    return pl.pallas_call(
        functools.partial(_ffn_body, residual=residual),
        grid_spec=grid_spec,
        out_shape=jax.ShapeDtypeStruct((n, d), F32),
        compiler_params=_cp("parallel", "arbitrary"),
        name="swiglu_res" if residual else "swiglu_moe",
    )(tile_expert, n_active, rows, g, wg_all, wu_all, wd_all)


def _route_body(x_ref, g_ref, wr_ref, o_ref):
    h = _rmsnorm(x_ref[...], g_ref[...])
    logits = _fdot(h, wr_ref[...])
    lane = lax.broadcasted_iota(jnp.int32, logits.shape, 1)
    lane_f = lane.astype(F32)
    neg = -jnp.inf
    logits = jnp.where(lane < N_EXPERTS, logits, neg)
    m1 = jnp.max(logits, axis=-1, keepdims=True)
    i1 = jnp.min(jnp.where(logits == m1, lane_f, 128.0), axis=-1, keepdims=True)
    rest = jnp.where(lane_f == i1, neg, logits)
    m2 = jnp.max(rest, axis=-1, keepdims=True)
    i2 = jnp.min(jnp.where(rest == m2, lane_f, 128.0), axis=-1, keepdims=True)
    e = jnp.exp(m2 - m1)
    p1 = 1.0 / (1.0 + e)
    p2 = e / (1.0 + e)
    out = jnp.where(lane == 0, i1, 0.0)
    out = jnp.where(lane == 1, i2, out)
    out = jnp.where(lane == 2, p1, out)
    out = jnp.where(lane == 3, p2, out)
    o_ref[...] = out


def _route(x2, g, wr_pad):
    n, d = x2.shape
    tm = min(TM_ROUTE, n)
    return pl.pallas_call(
        _route_body,
        grid=(n // tm,),
        in_specs=[pl.BlockSpec((tm, d), lambda i: (i, 0)),
                  pl.BlockSpec((1, d), lambda i: (0, 0)),
                  pl.BlockSpec((d, 128), lambda i: (0, 0))],
        out_specs=pl.BlockSpec((tm, 128), lambda i: (i, 0)),
        out_shape=jax.ShapeDtypeStruct((n, 128), F32),
        compiler_params=_cp("parallel"),
        name="router",
    )(x2, g, wr_pad)


def _row_copy(src_hbm, dst_vmem, src_row, dst_row, sem):
    return pltpu.make_async_copy(src_hbm.at[pl.ds(src_row, 1)], dst_vmem.at[pl.ds(dst_row, 1)], sem)


def _gather_rows(idx_ref, src_hbm, dst_vmem, sem):
    n_rows = dst_vmem.shape[0]

    def issue(r, carry):
        for p in range(2):
            _row_copy(src_hbm, dst_vmem, idx_ref[0, 0, 2 * r + p], 2 * r + p, sem).start(priority=p)
        return carry

    lax.fori_loop(0, n_rows // 2, issue, 0)
    pltpu.make_async_copy(src_hbm.at[pl.ds(0, n_rows)], dst_vmem, sem).wait()


def _gather_body(idx_ref, x_hbm, o_ref, sem):
    _gather_rows(idx_ref, x_hbm, o_ref, sem)


def _gather(x2, tok_of_slot):
    n_slots = tok_of_slot.shape[0]
    d = x2.shape[1]
    tm = TM_GATHER
    idx = tok_of_slot.reshape(n_slots // tm, 1, tm)
    return pl.pallas_call(
        _gather_body,
        grid=(n_slots // tm,),
        in_specs=[pl.BlockSpec((1, 1, tm), lambda i: (i, 0, 0), memory_space=pltpu.SMEM),
                  pl.BlockSpec(memory_space=pl.ANY)],
        out_specs=pl.BlockSpec((tm, d), lambda i: (i, 0)),
        out_shape=jax.ShapeDtypeStruct((n_slots, d), F32),
        scratch_shapes=[pltpu.SemaphoreType.DMA(())],
        compiler_params=_cp("arbitrary"),
        name="moe_gather",
    )(idx, x2)


def _combine_body(s1_ref, s2_ref, y_hbm, x_ref, info_ref, g_ref, o_ref, y1_ref, y2_ref, sem1, sem2, *,
                  final_norm):
    _gather_rows(s1_ref, y_hbm, y1_ref, sem1)
    _gather_rows(s2_ref, y_hbm, y2_ref, sem2)
    out = x_ref[...] + info_ref[:, 2:3] * y1_ref[...] + info_ref[:, 3:4] * y2_ref[...]
    if final_norm:
        out = _rmsnorm(out, g_ref[...])
    o_ref[...] = out


def _combine(y_sorted, x2, info, slot1, slot2, g_final, final_norm):
    n, d = x2.shape
    tm = min(TM_COMBINE, n)
    sspec = pl.BlockSpec((1, 1, tm), lambda i: (i, 0, 0), memory_space=pltpu.SMEM)
    return pl.pallas_call(
        functools.partial(_combine_body, final_norm=final_norm),
        grid=(n // tm,),
        in_specs=[sspec, sspec,
                  pl.BlockSpec(memory_space=pl.ANY),
                  pl.BlockSpec((tm, d), lambda i: (i, 0)),
                  pl.BlockSpec((tm, 128), lambda i: (i, 0)),
                  pl.BlockSpec((1, d), lambda i: (0, 0))],
        out_specs=pl.BlockSpec((tm, d), lambda i: (i, 0)),
        out_shape=jax.ShapeDtypeStruct((n, d), F32),
        scratch_shapes=[pltpu.VMEM((tm, d), F32), pltpu.VMEM((tm, d), F32),
                        pltpu.SemaphoreType.DMA(()), pltpu.SemaphoreType.DMA(())],
        compiler_params=_cp("arbitrary"),
        name="moe_combine",
    )(slot1.reshape(n // tm, 1, tm), slot2.reshape(n // tm, 1, tm), y_sorted, x2, info, g_final)


def _dispatch_plan(info, tm):
    n = info.shape[0]
    experts = info[:, 0:2].astype(jnp.int32)
    flat = experts.reshape(-1)
    onehot = (flat[:, None] == jnp.arange(N_EXPERTS, dtype=jnp.int32)[None, :]).astype(jnp.int32)
    csum = jnp.cumsum(onehot, axis=0)
    rank = jnp.sum(csum * onehot, axis=1) - 1
    counts = csum[-1]
    padded = ((counts + tm - 1) // tm) * tm
    ends = jnp.cumsum(padded)
    starts = ends - padded
    slot = (jnp.sum(starts[None, :] * onehot, axis=1) + rank).astype(jnp.int32)
    n_slots = 2 * n + N_EXPERTS * tm
    token = jnp.arange(2 * n, dtype=jnp.int32) // 2
    tok_of_slot = jnp.zeros((n_slots,), jnp.int32).at[slot].set(token)
    tile_start = jnp.arange(n_slots // tm, dtype=jnp.int32) * tm
    tile_expert = jnp.minimum(
        jnp.sum((tile_start[:, None] >= ends[None, :]).astype(jnp.int32), axis=1), N_EXPERTS - 1)
    n_active = (ends[-1] // tm).astype(jnp.int32).reshape(1)
    slots = slot.reshape(n, 2)
    return tok_of_slot, tile_expert.astype(jnp.int32), n_active, slots[:, 0], slots[:, 1]


def _prep_w_in(w_in):
    sl = lambda a, b: w_in[:, :, a:b]
    parts = [sl(4872, 8968), sl(0, 1024), sl(3840, 4864), sl(1024, 1536), sl(1536, 3072),
             sl(3328, 3840), sl(3072, 3328), sl(4864, 4872)]
    w = jnp.concatenate(parts, axis=-1)
    w = jnp.pad(w, ((0, 0), (0, 0), (0, Z_WIDTH - w.shape[-1])))
    return w.astype(BF16)


def kernel(x, norm_mix_g, w_in, a_ln_g, a_ln_b, a_w_s, a_b_s, b_w_pool, b_scale, c_mix, c_w0, c_w2, c_a0, c_a2, c_g2, c_k_k, c_k_a, c_r_k, c_ln_g, c_ln_b, d_conv_w, d_conv_b, d_dt_bias, d_a_log, d_skip, d_norm_g, w_branch, w_out, norm_ffn_g, ffn_w_gate, ffn_w_up, ffn_w_down, moe_router, moe_w_gate, moe_w_up, moe_w_down, norm_final_g):
    batch, seq, d = x.shape
    depth = w_in.shape[0]
    assert depth % 2 == 0, "the final rmsnorm is fused into the last routed layer's combine"
    n = batch * seq
    x2 = x.reshape(n, d)

    w_in_p = _prep_w_in(w_in)
    wb_all = w_branch.astype(BF16)
    wo_all = w_out.astype(BF16)
    ffn_wg = ffn_w_gate.astype(BF16)[:, None]
    ffn_wu = ffn_w_up.astype(BF16)[:, None]
    ffn_wd = ffn_w_down.astype(BF16)[:, None]
    moe_wg = moe_w_gate.astype(BF16)
    moe_wu = moe_w_up.astype(BF16)
    moe_wd = moe_w_down.astype(BF16)

    row = lambda v: v.reshape(1, -1)
    pad_lanes = lambda v: jnp.pad(v, (0, 128 - v.shape[0])).reshape(1, 128)
    dense_tiles = n // min(TM_FFN, n)
    dense_te = jnp.zeros((dense_tiles,), jnp.int32)
    dense_na = jnp.full((1,), dense_tiles, jnp.int32)

    for l in range(depth):
        z = _in_proj(x2, row(norm_mix_g[l]), w_in_p, l)
        y_a = _mixer_a(z, row(a_ln_g[l]), row(a_ln_b[l]), a_w_s[l], a_b_s[l].T, seq)
        y_b = _mixer_b(z, b_w_pool[l], row(b_scale[l]), batch, seq)
        w2p = jnp.pad(c_w2[l], ((0, 192), (0, 0)))
        a2p = jnp.pad(c_a2[l], ((64, 128), (0, 0)))
        g2p = jnp.pad(c_g2[l], ((128, 0), (0, 0)))
        y_c = _mixer_c(z, row(c_mix[l]), row(c_w0[l]), w2p, row(c_a0[l]), a2p, g2p,
                       row(c_k_k[l]), row(c_k_a[l]), row(c_r_k[l]), row(c_ln_g[l]), row(c_ln_b[l]),
                       batch, seq)
        y_d = _mixer_d(z, d_conv_w[l], row(d_conv_b[l]), pad_lanes(d_dt_bias[l]),
                       pad_lanes(d_a_log[l]), row(jnp.repeat(d_skip[l], HEAD)), row(d_norm_g[l]),
                       batch, seq)
        x2 = _merge((y_a, y_b, y_c, y_d), z, x2, wb_all, wo_all, l)

        g_ffn = row(norm_ffn_g[l])
        if l % 2 == 0:
            x2 = _ffn(x2, g_ffn, ffn_wg, ffn_wu, ffn_wd, l // 2, dense_te, dense_na,
                      min(TM_FFN, n), True)
        else:
            i = l // 2
            wr_pad = jnp.pad(moe_router[i], ((0, 0), (0, 128 - N_EXPERTS)))
            info = _route(x2, g_ffn, wr_pad)
            tok_of_slot, tile_expert, n_active, slot1, slot2 = _dispatch_plan(info, TM_MOE)
            rows = _gather(x2, tok_of_slot)
            y_sorted = _ffn(rows, g_ffn, moe_wg, moe_wu, moe_wd, i, tile_expert, n_active,
                            TM_MOE, False)
            x2 = _combine(y_sorted, x2, info, slot1, slot2, row(norm_final_g), l == depth - 1)
    return x2.reshape(batch, seq, d)
```

```python
import functools

import jax
import jax.numpy as jnp
from jax import lax
from jax.experimental import pallas as pl
from jax.experimental.pallas import tpu as pltpu

F32 = jnp.float32
BF16 = jnp.bfloat16
HIGHEST = lax.Precision.HIGHEST

EPS = 1e-6
GN_EPS = 64e-5
D_MODEL = 1024
N_HEADS = 8
HEAD = 64
WIDTH = 512
A_CHUNK = 128
D_CHUNK = 128
D_STATE = 128
C_CHUNK = 64
B_WINDOWS = (2, 4, 8, 16)
N_EXPERTS = 8
D_FF = 3584

Z_G, Z_A, Z_DXBC, Z_B, Z_R, Z_K, Z_V, Z_DZ, Z_LORA, Z_DT = (
    0, 4096, 5120, 6144, 6656, 7168, 7680, 8192, 8704, 8960)
Z_WIDTH = 9216

TM_PROJ = 1024
TN_PROJ = 1024
TM_A = 512
TM_B = 512
TM_C = 256
TM_D = 128
TM_MERGE = 256
TM_FFN = 512
TM_MOE = 512
TF_FFN = 1792
TM_ROUTE = 512
TM_GATHER = 256
TM_COMBINE = 256
VMEM_LIMIT = 56 * 1024 * 1024


def _cp(*sem):
    return pltpu.CompilerParams(dimension_semantics=sem, vmem_limit_bytes=VMEM_LIMIT)


def _bdot(a, b):
    return jnp.dot(a.astype(BF16), b.astype(BF16), preferred_element_type=F32)


def _fdot(a, b):
    return jnp.dot(a, b, preferred_element_type=F32, precision=HIGHEST)


def _dot_nt(a, b, precision=None):
    return lax.dot_general(a, b, (((1,), (1,)), ((), ())), preferred_element_type=F32,
                           precision=precision)


def _dot_tn(a, b, precision=None):
    return lax.dot_general(a, b, (((0,), (0,)), ((), ())), preferred_element_type=F32,
                           precision=precision)


def _sigmoid(x):
    return 1.0 / (1.0 + jnp.exp(-x))


def _silu(x):
    return x * _sigmoid(x)


def _softplus(x):
    return jnp.maximum(x, 0.0) + jnp.log(1.0 + jnp.exp(-jnp.abs(x)))


def _rmsnorm(x, g):
    return x * lax.rsqrt(jnp.mean(x * x, axis=-1, keepdims=True) + EPS) * g


def _inproj_body(x_ref, g_ref, w_ref, o_ref, h_ref):
    @pl.when(pl.program_id(1) == 0)
    def _():
        h_ref[...] = _rmsnorm(x_ref[...], g_ref[...]).astype(BF16)

    o_ref[...] = jnp.dot(h_ref[...], w_ref[...], preferred_element_type=F32).astype(o_ref.dtype)


def _in_proj(x2, g, w_all, layer):
    n, d = x2.shape
    zw = w_all.shape[-1]
    tm = min(TM_PROJ, n)
    tn = TN_PROJ
    return pl.pallas_call(
        _inproj_body,
        grid=(n // tm, zw // tn),
        in_specs=[pl.BlockSpec((tm, d), lambda i, j: (i, 0)),
                  pl.BlockSpec((1, d), lambda i, j: (0, 0)),
                  pl.BlockSpec((None, d, tn), lambda i, j: (layer, 0, j))],
        out_specs=pl.BlockSpec((tm, tn), lambda i, j: (i, j)),
        out_shape=jax.ShapeDtypeStruct((n, zw), BF16),
        scratch_shapes=[pltpu.VMEM((tm, d), BF16)],
        compiler_params=_cp("parallel", "arbitrary"),
        name="in_proj",
    )(x2, g, w_all)


def _mixa_body(uv_ref, lng_ref, lnb_ref, ws_ref, bs_ref, o_ref):
    tm = uv_ref.shape[0]
    row = lax.broadcasted_iota(jnp.int32, (A_CHUNK, A_CHUNK), 0)
    col = lax.broadcasted_iota(jnp.int32, (A_CHUNK, A_CHUNK), 1)
    ws = [jnp.where(row >= col, ws_ref[g], 0.0).astype(BF16) for g in range(4)]
    for c in range(tm // A_CHUNK):
        rows = slice(c * A_CHUNK, (c + 1) * A_CHUNK)
        uv = jax.nn.gelu(uv_ref[rows, :].astype(F32))
        u = uv[:, :WIDTH]
        v = uv[:, WIDTH:]
        mu = jnp.mean(v, axis=-1, keepdims=True)
        d = v - mu
        var = jnp.mean(d * d, axis=-1, keepdims=True)
        vn = d * lax.rsqrt(var + EPS) * lng_ref[...] + lnb_ref[...]
        for g in range(4):
            cols = slice(g * 128, (g + 1) * 128)
            sg = jnp.dot(ws[g], vn[:, cols].astype(BF16), preferred_element_type=F32)
            sg = sg + bs_ref[:, g:g + 1]
            o_ref[rows, cols] = (u[:, cols] * sg).astype(o_ref.dtype)


def _mixer_a(z, ln_g, ln_b, w_s, b_s_t, seq):
    n = z.shape[0]
    tm = min(TM_A, seq)
    return pl.pallas_call(
        _mixa_body,
        grid=(n // tm,),
        in_specs=[pl.BlockSpec((tm, 2 * WIDTH), lambda i: (i, Z_A // (2 * WIDTH))),
                  pl.BlockSpec((1, WIDTH), lambda i: (0, 0)),
                  pl.BlockSpec((1, WIDTH), lambda i: (0, 0)),
                  pl.BlockSpec((4, A_CHUNK, A_CHUNK), lambda i: (0, 0, 0)),
                  pl.BlockSpec((A_CHUNK, 4), lambda i: (0, 0))],
        out_specs=pl.BlockSpec((tm, WIDTH), lambda i: (i, 0)),
        out_shape=jax.ShapeDtypeStruct((n, WIDTH), BF16),
        compiler_params=_cp("parallel"),
        name="mixer_a",
    )(z, ln_g, ln_b, w_s, b_s_t)


B_PAD = 16


def _mixb_body(p_ref, w_ref, sc_ref, o_ref, pad_ref):
    t = pl.program_id(1)
    tm = p_ref.shape[0]

    @pl.when(t == 0)
    def _():
        pad_ref[0:B_PAD, :] = jnp.zeros((B_PAD, WIDTH), F32)

    @pl.when(t > 0)
    def _():
        pad_ref[0:B_PAD, :] = pad_ref[tm:tm + B_PAD, :]

    pad_ref[B_PAD:B_PAD + tm, :] = p_ref[...].astype(F32)
    pos = (t * tm + lax.broadcasted_iota(jnp.int32, (tm, 1), 0)).astype(F32)
    for g, win in enumerate(B_WINDOWS):
        cols = slice(g * 128, (g + 1) * 128)
        p_g = pad_ref[B_PAD:B_PAD + tm, cols]
        acc = p_g
        for j in range(1, win):
            acc = acc + pad_ref[B_PAD - j:B_PAD - j + tm, cols]
        pooled = acc / jnp.minimum(pos + 1.0, float(win)) - p_g
        y = _bdot(pooled, w_ref[g]) * sc_ref[:, cols]
        o_ref[:, cols] = y.astype(o_ref.dtype)


def _mixer_b(z, w_pool, scale, batch, seq):
    n = z.shape[0]
    tm = min(TM_B, seq)
    nt = seq // tm
    return pl.pallas_call(
        _mixb_body,
        grid=(batch, nt),
        in_specs=[pl.BlockSpec((tm, WIDTH), lambda b, t: (b * nt + t, Z_B // WIDTH)),
                  pl.BlockSpec((4, 128, 128), lambda b, t: (0, 0, 0)),
                  pl.BlockSpec((1, WIDTH), lambda b, t: (0, 0))],
        out_specs=pl.BlockSpec((tm, WIDTH), lambda b, t: (b * nt + t, 0)),
        out_shape=jax.ShapeDtypeStruct((n, WIDTH), BF16),
        scratch_shapes=[pltpu.VMEM((tm + B_PAD, WIDTH), F32)],
        compiler_params=_cp("parallel", "arbitrary"),
        name="mixer_b",
    )(z, w_pool, scale)


D_PAD = 8


def _mixd_body(dz_ref, xbc_ref, dt_ref, cw_ref, cb_ref, dtb_ref, alog_ref, dsk_ref, ng_ref,
               o_ref, pad_ref, h_ref, y_ref):
    t = pl.program_id(1)
    tm = xbc_ref.shape[0]
    L = D_CHUNK

    @pl.when(t == 0)
    def _():
        pad_ref[0:D_PAD, :] = jnp.zeros((D_PAD, 2 * WIDTH), F32)
        h_ref[...] = jnp.zeros(h_ref.shape, F32)

    @pl.when(t > 0)
    def _():
        pad_ref[0:D_PAD, :] = pad_ref[tm:tm + D_PAD, :]

    pad_ref[D_PAD:D_PAD + tm, :] = xbc_ref[...].astype(F32)

    row = lax.broadcasted_iota(jnp.int32, (L, L), 0)
    col = lax.broadcasted_iota(jnp.int32, (L, L), 1)
    causal = row >= col
    tril = jnp.where(causal, 1.0, 0.0).astype(F32)
    left = col < HEAD
    top = row < HEAD
    a_head = -jnp.exp(alog_ref[...])

    for c in range(tm // L):
        base = D_PAD + c * L
        rows = slice(c * L, (c + 1) * L)
        conv = cb_ref[...]
        for k in range(4):
            conv = conv + cw_ref[k:k + 1, :] * pad_ref[base - 3 + k:base - 3 + k + L, :]
        xbc = _silu(conv)
        dt = _softplus(dt_ref[rows, :].astype(F32) + dtb_ref[...])
        cs = _fdot(tril, dt * a_head)
        cs_t = cs.T
        last = cs[L - 1:L, :]
        e_cs = jnp.exp(cs)
        e_end = jnp.exp(last - cs)
        e_last = jnp.exp(last)
        for g in range(2):
            b_g = xbc[:, WIDTH + g * D_STATE:WIDTH + (g + 1) * D_STATE].astype(BF16)
            c_g = xbc[:, WIDTH + (2 + g) * D_STATE:WIDTH + (3 + g) * D_STATE].astype(BF16)
            cb = _dot_nt(c_g, b_g)
            for j in range(2):
                pair = g * 2 + j
                h0, h1 = 2 * pair, 2 * pair + 1
                cols = slice(pair * 128, (pair + 1) * 128)
                xp = xbc[:, cols]
                xd = xp * jnp.where(left, dt[:, h0:h0 + 1], dt[:, h1:h1 + 1])
                xd_b = xd.astype(BF16)
                ys = []
                for h in (h0, h1):
                    seg = cs[:, h:h + 1] - cs_t[h:h + 1, :]
                    m = jnp.exp(jnp.where(causal, seg, -jnp.inf)) * cb
                    ys.append(jnp.dot(m.astype(BF16), xd_b, preferred_element_type=F32))
                y_diag = jnp.where(left, ys[0], ys[1])
                hp = h_ref[pair]
                y_off = _dot_nt(c_g, hp.astype(BF16))
                y_off = y_off * jnp.where(left, e_cs[:, h0:h0 + 1], e_cs[:, h1:h1 + 1])
                xdd = xd * jnp.where(left, e_end[:, h0:h0 + 1], e_end[:, h1:h1 + 1])
                dec = jnp.where(top, e_last[:, h0:h0 + 1], e_last[:, h1:h1 + 1])
                h_ref[pair] = hp * dec + _dot_tn(xdd.astype(BF16), b_g)
                y = y_diag + y_off + dsk_ref[:, cols] * xp
                y_ref[:, cols] = y * _silu(dz_ref[rows, cols].astype(F32))
        o_ref[rows, :] = _rmsnorm(y_ref[...], ng_ref[...]).astype(o_ref.dtype)


def _mixer_d(z, conv_w, conv_b, dt_bias, a_log, d_skip, norm_g, batch, seq):
    n = z.shape[0]
    tm = min(TM_D, seq)
    nt = seq // tm
    return pl.pallas_call(
        _mixd_body,
        grid=(batch, nt),
        in_specs=[pl.BlockSpec((tm, WIDTH), lambda b, t: (b * nt + t, Z_DZ // WIDTH)),
                  pl.BlockSpec((tm, 2 * WIDTH), lambda b, t: (b * nt + t, Z_DXBC // (2 * WIDTH))),
                  pl.BlockSpec((tm, 128), lambda b, t: (b * nt + t, Z_DT // 128)),
                  pl.BlockSpec((4, 2 * WIDTH), lambda b, t: (0, 0)),
                  pl.BlockSpec((1, 2 * WIDTH), lambda b, t: (0, 0)),
                  pl.BlockSpec((1, 128), lambda b, t: (0, 0)),
                  pl.BlockSpec((1, 128), lambda b, t: (0, 0)),
                  pl.BlockSpec((1, WIDTH), lambda b, t: (0, 0)),
                  pl.BlockSpec((1, WIDTH), lambda b, t: (0, 0))],
        out_specs=pl.BlockSpec((tm, WIDTH), lambda b, t: (b * nt + t, 0)),
        out_shape=jax.ShapeDtypeStruct((n, WIDTH), BF16),
        scratch_shapes=[pltpu.VMEM((tm + D_PAD, 2 * WIDTH), F32),
                        pltpu.VMEM((N_HEADS // 2, 2 * HEAD, D_STATE), F32),
                        pltpu.VMEM((D_CHUNK, WIDTH), F32)],
        compiler_params=_cp("parallel", "arbitrary"),
        name="mixer_d",
    )(z, z, z, conv_w, conv_b, dt_bias, a_log, d_skip, norm_g)


C_PAD = 8


def _head_sum(x, ones_blk):
    hi = x.astype(BF16)
    lo = (x - hi.astype(F32)).astype(BF16)
    return (jnp.dot(hi, ones_blk, preferred_element_type=F32)
            + jnp.dot(lo, ones_blk, preferred_element_type=F32))


def _mixc_body(r_ref, k_ref, v_ref, lo_ref, mr_ref, mk_ref, mv_ref, ml_ref,
               w0_ref, w2_ref, a0_ref, a2_ref, g2_ref, kk_ref, ka_ref, rk_ref, lng_ref, lnb_ref,
               o_ref,
               rpad, kpad, vpad, lpad, st_ref, r_s, k_s, v_s, a_s, b_s, d_s, g_s, o_s):
    t = pl.program_id(1)
    nq, tm = r_ref.shape[0], r_ref.shape[1]
    C = C_CHUNK

    pads = ((r_ref, rpad), (k_ref, kpad), (v_ref, vpad), (lo_ref, lpad))

    @pl.when(t == 0)
    def _():
        for _, pad in pads:
            pad[:, 0:C_PAD, :] = jnp.zeros((nq, C_PAD, pad.shape[2]), F32)
        st_ref[...] = jnp.zeros(st_ref.shape, F32)

    @pl.when(t > 0)
    def _():
        for _, pad in pads:
            pad[:, 0:C_PAD, :] = pad[:, tm:tm + C_PAD, :]

    for src, pad in pads:
        pad[:, C_PAD:C_PAD + tm, :] = src[...].astype(F32)

    hrow = lax.broadcasted_iota(jnp.int32, (WIDTH, WIDTH), 0) // HEAD
    hcol = lax.broadcasted_iota(jnp.int32, (WIDTH, WIDTH), 1) // HEAD
    ones_blk = jnp.where(hrow == hcol, 1.0, 0.0).astype(BF16)

    for q in range(nq):
        def shifted(pad, mix_ref):
            x = pad[q, C_PAD:C_PAD + tm, :]
            return x + (pad[q, C_PAD - 1:C_PAD - 1 + tm, :] - x) * mix_ref[...]

        r = shifted(rpad, mr_ref)
        k = shifted(kpad, mk_ref)
        v = shifted(vpad, mv_ref)
        lo = shifted(lpad, ml_ref)
        w_log = -_softplus(-(w0_ref[...] + _bdot(jnp.tanh(lo), w2_ref[...]))) - 0.5
        a = _sigmoid(a0_ref[...] + _bdot(lo, a2_ref[...]))
        g_s[q] = _bdot(_sigmoid(lo), g2_ref[...])
        kk = k * kk_ref[...]
        kk = kk / jnp.maximum(jnp.sqrt(_head_sum(kk * kk, ones_blk)), 1e-12)
        r_s[q] = r
        k_s[q] = k * (1.0 + (a - 1.0) * ka_ref[...])
        v_s[q] = v
        a_s[q] = kk
        b_s[q] = kk * a
        d_s[q] = -jnp.exp(w_log)

    row = lax.broadcasted_iota(jnp.int32, (C, C), 0)
    col = lax.broadcasted_iota(jnp.int32, (C, C), 1)
    tril = jnp.where(row >= col, 1.0, 0.0).astype(F32)
    prow = lax.broadcasted_iota(jnp.int32, (C, 2 * HEAD), 0)
    pcol = lax.broadcasted_iota(jnp.int32, (C, 2 * HEAD), 1)
    left = pcol < HEAD
    strict = prow > (pcol & (HEAD - 1))
    incl = prow >= (pcol & (HEAD - 1))
    brow = lax.broadcasted_iota(jnp.int32, (2 * C, 2 * C), 0)
    bcol = lax.broadcasted_iota(jnp.int32, (2 * C, 2 * C), 1)
    eye = jnp.where(brow == bcol, 1.0, 0.0).astype(F32)

    def blk(x):
        return jnp.concatenate([jnp.where(left, x, 0.0), jnp.where(left, 0.0, x)], axis=0)

    def chunk(c, carry):
        rows = pl.ds(pl.multiple_of(c * C, C), C)
        n_pairs = N_HEADS // 2
        pairs = range(nq * n_pairs)
        pcs = [slice((i % n_pairs) * 2 * HEAD, (i % n_pairs + 1) * 2 * HEAD) for i in pairs]
        ar, b_k, k_k, v_k, p_end = [], [], [], [], []
        for q in range(nq):
            ld = d_s[q, rows, :]
            cs = _fdot(tril, ld)
            p_in = jnp.exp(cs)
            p_inv = jnp.exp(-cs)
            rt = r_s[q, rows, :] * p_in
            kt = k_s[q, rows, :] * p_inv
            bt = b_s[q, rows, :] * p_inv
            at = a_s[q, rows, :] * jnp.exp(cs - ld)
            vv = v_s[q, rows, :]
            for pc in pcs[:n_pairs]:
                ar.append(jnp.concatenate([at[:, pc], rt[:, pc]], axis=0).astype(BF16))
                b_k.append(blk(bt[:, pc]).astype(BF16))
                k_k.append(blk(kt[:, pc]).astype(BF16))
                v_k.append(blk(vv[:, pc]).astype(BF16))
                p_end.append(p_in[C - 1:C, pc])
        s = [st_ref[p] for p in pairs]
        m_b = [_dot_nt(ar[p], b_k[p]) for p in pairs]
        m_k = [_dot_nt(ar[p], k_k[p]) for p in pairs]
        m_s = [_dot_nt(ar[p], s[p].astype(BF16)) for p in pairs]
        xp = [blk(-jnp.where(strict, m_b[p][:C], 0.0)).astype(BF16) for p in pairs]
        inv = [eye + xp[p].astype(F32) for p in pairs]
        rhs = [m_s[p][:C] + _bdot(jnp.where(strict, m_k[p][:C], 0.0), v_k[p]) for p in pairs]
        for _ in range(5):
            xp = [_bdot(xp[p], xp[p]).astype(BF16) for p in pairs]
            inv = [inv[p] + _bdot(inv[p], xp[p]) for p in pairs]
        u_k = [(-_bdot(inv[p], blk(rhs[p]))).astype(BF16) for p in pairs]
        for p in pairs:
            uv = jnp.concatenate([u_k[p], v_k[p]], axis=0)
            a_rr = jnp.concatenate([jnp.where(incl, m_b[p][C:], 0.0),
                                    jnp.where(incl, m_k[p][C:], 0.0)], axis=1)
            o_s[p // n_pairs, rows, pcs[p]] = m_s[p][C:] + _bdot(a_rr, uv)
            bk = jnp.concatenate([b_k[p], k_k[p]], axis=0)
            st_ref[p] = (s[p] + _dot_tn(uv, bk)) * p_end[p]
        return carry

    lax.fori_loop(0, tm // C, chunk, 0)

    for q in range(nq):
        o = o_s[q]
        mu = _head_sum(o, ones_blk) * (1.0 / HEAD)
        d = o - mu
        var = _head_sum(d * d, ones_blk) * (1.0 / HEAD)
        on = d * lax.rsqrt(var + GN_EPS) * lng_ref[...] + lnb_ref[...]
        bonus = _head_sum(r_s[q] * k_s[q] * rk_ref[...], ones_blk) * v_s[q]
        o_ref[q] = ((on + bonus) * g_s[q]).astype(o_ref.dtype)


C_SEQS = 4


def _mixer_c(z, mix, w0, w2p, a0, a2p, g2p, k_k, k_a, r_k, ln_g, ln_b, batch, seq):
    n = z.shape[0]
    tm = min(TM_C, seq)
    nt = seq // tm
    nq = C_SEQS if batch % C_SEQS == 0 else 1
    lw = 256
    z3 = z.reshape(batch, seq, z.shape[1])
    tok = lambda off, w: pl.BlockSpec((nq, tm, w), lambda b, t: (b, t, off // w))
    par = lambda rows, w: pl.BlockSpec((rows, w), lambda b, t: (0, 0))
    vm = lambda w: pltpu.VMEM((nq, tm, w), F32)
    pad = lambda w: pltpu.VMEM((nq, tm + C_PAD, w), F32)
    out = pl.pallas_call(
        _mixc_body,
        grid=(batch // nq, nt),
        in_specs=[tok(Z_R, WIDTH), tok(Z_K, WIDTH), tok(Z_V, WIDTH), tok(Z_LORA, lw),
                  par(1, WIDTH), par(1, WIDTH), par(1, WIDTH), par(1, lw),
                  par(1, WIDTH), par(lw, WIDTH), par(1, WIDTH), par(lw, WIDTH), par(lw, WIDTH),
                  par(1, WIDTH), par(1, WIDTH), par(1, WIDTH), par(1, WIDTH), par(1, WIDTH)],
        out_specs=pl.BlockSpec((nq, tm, WIDTH), lambda b, t: (b, t, 0)),
        out_shape=jax.ShapeDtypeStruct((batch, seq, WIDTH), BF16),
        scratch_shapes=[pad(WIDTH), pad(WIDTH), pad(WIDTH), pad(lw),
                        pltpu.VMEM((nq * N_HEADS // 2, 2 * HEAD, 2 * HEAD), F32),
                        vm(WIDTH), vm(WIDTH), vm(WIDTH), vm(WIDTH), vm(WIDTH), vm(WIDTH), vm(WIDTH),
                        vm(WIDTH)],
        compiler_params=_cp("parallel", "arbitrary"),
        name="mixer_c",
    )(z3, z3, z3, z3, mix[:, 0:WIDTH], mix[:, WIDTH:2 * WIDTH], mix[:, 2 * WIDTH:3 * WIDTH],
      mix[:, 3 * WIDTH:], w0, w2p, a0, a2p, g2p, k_k, k_a, r_k, ln_g, ln_b)
    return out.reshape(n, WIDTH)


def _merge_body(ya_ref, yb_ref, yc_ref, yd_ref, zg_ref, x_ref, wb_ref, wo_ref, o_ref):
    merged = None
    for kk, y_ref in enumerate((ya_ref, yb_ref, yc_ref, yd_ref)):
        proj = jnp.dot(y_ref[...], wb_ref[kk], preferred_element_type=F32)
        gate = _sigmoid(zg_ref[:, kk * D_MODEL:(kk + 1) * D_MODEL].astype(F32))
        merged = gate * proj if merged is None else merged + gate * proj
    o_ref[...] = x_ref[...] + jnp.dot(merged.astype(BF16), wo_ref[...], preferred_element_type=F32)


def _merge(ys, z, x2, wb_all, wo_all, layer):
    n = x2.shape[0]
    tm = min(TM_MERGE, n)
    yspec = pl.BlockSpec((tm, WIDTH), lambda i: (i, 0))
    return pl.pallas_call(
        _merge_body,
        grid=(n // tm,),
        in_specs=[yspec, yspec, yspec, yspec,
                  pl.BlockSpec((tm, 4 * D_MODEL), lambda i: (i, Z_G // (4 * D_MODEL))),
                  pl.BlockSpec((tm, D_MODEL), lambda i: (i, 0)),
                  pl.BlockSpec((None, 4, WIDTH, D_MODEL), lambda i: (layer, 0, 0, 0)),
                  pl.BlockSpec((None, D_MODEL, D_MODEL), lambda i: (layer, 0, 0))],
        out_specs=pl.BlockSpec((tm, D_MODEL), lambda i: (i, 0)),
        out_shape=jax.ShapeDtypeStruct((n, D_MODEL), F32),
        compiler_params=_cp("parallel"),
        name="merge",
    )(*ys, z, x2, wb_all, wo_all)


ROW_TILE = 8


def _row_tile_chunk(ref, s, rows):
    return ref[pl.ds(s, rows, stride=ROW_TILE), :]


def _ffn_body(te_ref, na_ref, x_ref, g_ref, wg_ref, wu_ref, wd_ref, o_ref, h_ref, acc_ref, *,
              residual, row_tiles):
    i = pl.program_id(0)
    j = pl.program_id(1)
    active = i < na_ref[0]
    tm, d = h_ref.shape

    @pl.when(jnp.logical_and(active, j == 0))
    def _():
        if row_tiles:
            ss = jnp.zeros((tm, 1), F32)
            for s in range(ROW_TILE):
                xs = _row_tile_chunk(x_ref, s, tm)
                ss = ss + jnp.sum(xs * xs, axis=-1, keepdims=True)
            scale = lax.rsqrt(ss * (1.0 / d) + EPS)
            for s in range(ROW_TILE):
                cols = slice(s * 128, (s + 1) * 128)
                h_ref[:, cols] = (_row_tile_chunk(x_ref, s, tm) * scale * g_ref[:, cols]).astype(BF16)
        else:
            h_ref[...] = _rmsnorm(x_ref[...], g_ref[...]).astype(BF16)
        acc_ref[...] = jnp.zeros(acc_ref.shape, F32)

    @pl.when(active)
    def _():
        h = h_ref[...]
        gate = jnp.dot(h, wg_ref[...], preferred_element_type=F32)
        up = jnp.dot(h, wu_ref[...], preferred_element_type=F32)
        act = (_silu(gate) * up).astype(BF16)
        acc_ref[...] += jnp.dot(act, wd_ref[...], preferred_element_type=F32)

    @pl.when(j == pl.num_programs(1) - 1)
    def _():
        @pl.when(active)
        def _():
            if row_tiles:
                for s in range(ROW_TILE):
                    o_ref[pl.ds(s, tm, stride=ROW_TILE), :] = acc_ref[:, s * 128:(s + 1) * 128]
            else:
                o_ref[...] = (x_ref[...] + acc_ref[...]) if residual else acc_ref[...]

        @pl.when(jnp.logical_not(active))
        def _():
            o_ref[...] = jnp.zeros(o_ref.shape, F32)


def _ffn(rows, g, wg_all, wu_all, wd_all, layer_idx, tile_expert, n_active, tm, residual, row_tiles):
    d = wg_all.shape[-2]
    n = rows.shape[0] // ROW_TILE if row_tiles else rows.shape[0]
    d_ff = wg_all.shape[-1]
    tf = TF_FFN
    assert not (row_tiles and residual) and d == ROW_TILE * 128
    io_block = (tm * ROW_TILE, 128) if row_tiles else (tm, d)
    grid_spec = pltpu.PrefetchScalarGridSpec(
        num_scalar_prefetch=2,
        grid=(n // tm, d_ff // tf),
        in_specs=[pl.BlockSpec(io_block, lambda i, j, te, na: (i, 0)),
                  pl.BlockSpec((1, d), lambda i, j, te, na: (0, 0)),
                  pl.BlockSpec((None, None, d, tf), lambda i, j, te, na: (layer_idx, te[i], 0, j)),
                  pl.BlockSpec((None, None, d, tf), lambda i, j, te, na: (layer_idx, te[i], 0, j)),
                  pl.BlockSpec((None, None, tf, d), lambda i, j, te, na: (layer_idx, te[i], j, 0))],
        out_specs=pl.BlockSpec(io_block, lambda i, j, te, na: (i, 0)),
        scratch_shapes=[pltpu.VMEM((tm, d), BF16), pltpu.VMEM((tm, d), F32)],
    )
    return pl.pallas_call(
        functools.partial(_ffn_body, residual=residual, row_tiles=row_tiles),
        grid_spec=grid_spec,
        out_shape=jax.ShapeDtypeStruct(rows.shape, F32),
        compiler_params=_cp("parallel", "arbitrary"),
        name="swiglu_res" if residual else "swiglu_moe",
    )(tile_expert, n_active, rows, g, wg_all, wu_all, wd_all)


def _route_body(x_ref, g_ref, wr_ref, o_ref):
    h = _rmsnorm(x_ref[...], g_ref[...])
    logits = _fdot(h, wr_ref[...])
    lane = lax.broadcasted_iota(jnp.int32, logits.shape, 1)
    lane_f = lane.astype(F32)
    neg = -jnp.inf
    logits = jnp.where(lane < N_EXPERTS, logits, neg)
    m1 = jnp.max(logits, axis=-1, keepdims=True)
    i1 = jnp.min(jnp.where(logits == m1, lane_f, 128.0), axis=-1, keepdims=True)
    rest = jnp.where(lane_f == i1, neg, logits)
    m2 = jnp.max(rest, axis=-1, keepdims=True)
    i2 = jnp.min(jnp.where(rest == m2, lane_f, 128.0), axis=-1, keepdims=True)
    e = jnp.exp(m2 - m1)
    p1 = 1.0 / (1.0 + e)
    p2 = e / (1.0 + e)
    out = jnp.where(lane == 0, i1, 0.0)
    out = jnp.where(lane == 1, i2, out)
    out = jnp.where(lane == 2, p1, out)
    out = jnp.where(lane == 3, p2, out)
    o_ref[...] = out


def _route(x2, g, wr_pad):
    n, d = x2.shape
    tm = min(TM_ROUTE, n)
    return pl.pallas_call(
        _route_body,
        grid=(n // tm,),
        in_specs=[pl.BlockSpec((tm, d), lambda i: (i, 0)),
                  pl.BlockSpec((1, d), lambda i: (0, 0)),
                  pl.BlockSpec((d, 128), lambda i: (0, 0))],
        out_specs=pl.BlockSpec((tm, 128), lambda i: (i, 0)),
        out_shape=jax.ShapeDtypeStruct((n, 128), F32),
        compiler_params=_cp("parallel"),
        name="router",
    )(x2, g, wr_pad)


def _row_tile_copy(src_hbm, dst_vmem, src_row, dst_row, sem):
    src = pl.ds(pl.multiple_of(src_row * ROW_TILE, ROW_TILE), ROW_TILE)
    dst = pl.ds(pl.multiple_of(dst_row * ROW_TILE, ROW_TILE), ROW_TILE)
    return pltpu.make_async_copy(src_hbm.at[src], dst_vmem.at[dst], sem)


def _gather_rows(idx_ref, src_hbm, dst_vmem, sem):
    n_rows = dst_vmem.shape[0] // ROW_TILE

    def issue(r, carry):
        for p in range(2):
            _row_tile_copy(src_hbm, dst_vmem, idx_ref[0, 0, 2 * r + p], 2 * r + p, sem).start(priority=p)
        return carry

    lax.fori_loop(0, n_rows // 2, issue, 0)
    pltpu.make_async_copy(src_hbm.at[pl.ds(0, n_rows * ROW_TILE)], dst_vmem, sem).wait()


def _gather_body(idx_ref, x_hbm, o_ref, sem):
    _gather_rows(idx_ref, x_hbm, o_ref, sem)


def _gather(x_tiles, tok_of_slot):
    n_slots = tok_of_slot.shape[0]
    tm = TM_GATHER
    idx = tok_of_slot.reshape(n_slots // tm, 1, tm)
    return pl.pallas_call(
        _gather_body,
        grid=(n_slots // tm,),
        in_specs=[pl.BlockSpec((1, 1, tm), lambda i: (i, 0, 0), memory_space=pltpu.SMEM),
                  pl.BlockSpec(memory_space=pl.ANY)],
        out_specs=pl.BlockSpec((tm * ROW_TILE, 128), lambda i: (i, 0)),
        out_shape=jax.ShapeDtypeStruct((n_slots * ROW_TILE, 128), F32),
        scratch_shapes=[pltpu.SemaphoreType.DMA(())],
        compiler_params=_cp("arbitrary"),
        name="moe_gather",
    )(idx, x_tiles)


def _combine_body(s1_ref, s2_ref, y_hbm, x_ref, info_ref, g_ref, o_ref, y1_ref, y2_ref, sem1, sem2, *,
                  final_norm):
    tm = x_ref.shape[0]
    _gather_rows(s1_ref, y_hbm, y1_ref, sem1)
    _gather_rows(s2_ref, y_hbm, y2_ref, sem2)
    p1 = info_ref[:, 2:3]
    p2 = info_ref[:, 3:4]
    for s in range(ROW_TILE):
        cols = slice(s * 128, (s + 1) * 128)
        o_ref[:, cols] = (x_ref[:, cols] + p1 * _row_tile_chunk(y1_ref, s, tm)
                          + p2 * _row_tile_chunk(y2_ref, s, tm))
    if final_norm:
        o_ref[...] = _rmsnorm(o_ref[...], g_ref[...])


def _combine(y_tiles, x2, info, slot1, slot2, g_final, final_norm):
    n, d = x2.shape
    tm = min(TM_COMBINE, n)
    sspec = pl.BlockSpec((1, 1, tm), lambda i: (i, 0, 0), memory_space=pltpu.SMEM)
    return pl.pallas_call(
        functools.partial(_combine_body, final_norm=final_norm),
        grid=(n // tm,),
        in_specs=[sspec, sspec,
                  pl.BlockSpec(memory_space=pl.ANY),
                  pl.BlockSpec((tm, d), lambda i: (i, 0)),
                  pl.BlockSpec((tm, 128), lambda i: (i, 0)),
                  pl.BlockSpec((1, d), lambda i: (0, 0))],
        out_specs=pl.BlockSpec((tm, d), lambda i: (i, 0)),
        out_shape=jax.ShapeDtypeStruct((n, d), F32),
        scratch_shapes=[pltpu.VMEM((tm * ROW_TILE, 128), F32), pltpu.VMEM((tm * ROW_TILE, 128), F32),
                        pltpu.SemaphoreType.DMA(()), pltpu.SemaphoreType.DMA(())],
        compiler_params=_cp("arbitrary"),
        name="moe_combine",
    )(slot1.reshape(n // tm, 1, tm), slot2.reshape(n // tm, 1, tm), y_tiles, x2, info, g_final)


def _dispatch_plan(info, tm):
    n = info.shape[0]
    experts = info[:, 0:2].astype(jnp.int32)
    flat = experts.reshape(-1)
    onehot = (flat[:, None] == jnp.arange(N_EXPERTS, dtype=jnp.int32)[None, :]).astype(jnp.int32)
    csum = jnp.cumsum(onehot, axis=0)
    rank = jnp.sum(csum * onehot, axis=1) - 1
    counts = csum[-1]
    padded = ((counts + tm - 1) // tm) * tm
    ends = jnp.cumsum(padded)
    starts = ends - padded
    slot = (jnp.sum(starts[None, :] * onehot, axis=1) + rank).astype(jnp.int32)
    n_slots = 2 * n + N_EXPERTS * tm
    token = jnp.arange(2 * n, dtype=jnp.int32) // 2
    tok_of_slot = jnp.zeros((n_slots,), jnp.int32).at[slot].set(token)
    tile_start = jnp.arange(n_slots // tm, dtype=jnp.int32) * tm
    tile_expert = jnp.minimum(
        jnp.sum((tile_start[:, None] >= ends[None, :]).astype(jnp.int32), axis=1), N_EXPERTS - 1)
    n_active = (ends[-1] // tm).astype(jnp.int32).reshape(1)
    slots = slot.reshape(n, 2)
    return tok_of_slot, tile_expert.astype(jnp.int32), n_active, slots[:, 0], slots[:, 1]


def _prep_w_in(w_in):
    sl = lambda a, b: w_in[:, :, a:b]
    parts = [sl(4872, 8968), sl(0, 1024), sl(3840, 4864), sl(1024, 1536), sl(1536, 3072),
             sl(3328, 3840), sl(3072, 3328), sl(4864, 4872)]
    w = jnp.concatenate(parts, axis=-1)
    w = jnp.pad(w, ((0, 0), (0, 0), (0, Z_WIDTH - w.shape[-1])))
    return w.astype(BF16)


def kernel(x, norm_mix_g, w_in, a_ln_g, a_ln_b, a_w_s, a_b_s, b_w_pool, b_scale, c_mix, c_w0, c_w2, c_a0, c_a2, c_g2, c_k_k, c_k_a, c_r_k, c_ln_g, c_ln_b, d_conv_w, d_conv_b, d_dt_bias, d_a_log, d_skip, d_norm_g, w_branch, w_out, norm_ffn_g, ffn_w_gate, ffn_w_up, ffn_w_down, moe_router, moe_w_gate, moe_w_up, moe_w_down, norm_final_g):
    batch, seq, d = x.shape
    depth = w_in.shape[0]
    assert depth % 2 == 0, "the final rmsnorm is fused into the last routed layer's combine"
    n = batch * seq
    x2 = x.reshape(n, d)

    w_in_p = _prep_w_in(w_in)
    wb_all = w_branch.astype(BF16)
    wo_all = w_out.astype(BF16)
    ffn_wg = ffn_w_gate.astype(BF16)[:, None]
    ffn_wu = ffn_w_up.astype(BF16)[:, None]
    ffn_wd = ffn_w_down.astype(BF16)[:, None]
    moe_wg = moe_w_gate.astype(BF16)
    moe_wu = moe_w_up.astype(BF16)
    moe_wd = moe_w_down.astype(BF16)

    row = lambda v: v.reshape(1, -1)
    pad_lanes = lambda v: jnp.pad(v, (0, 128 - v.shape[0])).reshape(1, 128)
    dense_tiles = n // min(TM_FFN, n)
    dense_te = jnp.zeros((dense_tiles,), jnp.int32)
    dense_na = jnp.full((1,), dense_tiles, jnp.int32)

    for l in range(depth):
        z = _in_proj(x2, row(norm_mix_g[l]), w_in_p, l)
        y_a = _mixer_a(z, row(a_ln_g[l]), row(a_ln_b[l]), a_w_s[l], a_b_s[l].T, seq)
        y_b = _mixer_b(z, b_w_pool[l], row(b_scale[l]), batch, seq)
        w2p = jnp.pad(c_w2[l], ((0, 192), (0, 0)))
        a2p = jnp.pad(c_a2[l], ((64, 128), (0, 0)))
        g2p = jnp.pad(c_g2[l], ((128, 0), (0, 0)))
        y_c = _mixer_c(z, row(c_mix[l]), row(c_w0[l]), w2p, row(c_a0[l]), a2p, g2p,
                       row(c_k_k[l]), row(c_k_a[l]), row(c_r_k[l]), row(c_ln_g[l]), row(c_ln_b[l]),
                       batch, seq)
        y_d = _mixer_d(z, d_conv_w[l], row(d_conv_b[l]), pad_lanes(d_dt_bias[l]),
                       pad_lanes(d_a_log[l]), row(jnp.repeat(d_skip[l], HEAD)), row(d_norm_g[l]),
                       batch, seq)
        x2 = _merge((y_a, y_b, y_c, y_d), z, x2, wb_all, wo_all, l)

        g_ffn = row(norm_ffn_g[l])
        if l % 2 == 0:
            x2 = _ffn(x2, g_ffn, ffn_wg, ffn_wu, ffn_wd, l // 2, dense_te, dense_na,
                      min(TM_FFN, n), True, False)
        else:
            i = l // 2
            wr_pad = jnp.pad(moe_router[i], ((0, 0), (0, 128 - N_EXPERTS)))
            info = _route(x2, g_ffn, wr_pad)
            tok_of_slot, tile_expert, n_active, slot1, slot2 = _dispatch_plan(info, TM_MOE)
            rows = _gather(x2.reshape(n * ROW_TILE, 128), tok_of_slot)
            y_tiles = _ffn(rows, g_ffn, moe_wg, moe_wu, moe_wd, i, tile_expert, n_active,
                           TM_MOE, False, True)
            x2 = _combine(y_tiles, x2, info, slot1, slot2, row(norm_final_g), l == depth - 1)
    return x2.reshape(batch, seq, d)
```

```python
import functools

import jax
import jax.numpy as jnp
from jax import lax
from jax.experimental import pallas as pl
from jax.experimental.pallas import tpu as pltpu

F32 = jnp.float32
BF16 = jnp.bfloat16
HIGHEST = lax.Precision.HIGHEST

EPS = 1e-6
GN_EPS = 64e-5
D_MODEL = 1024
N_HEADS = 8
HEAD = 64
WIDTH = 512
A_CHUNK = 128
D_CHUNK = 128
D_STATE = 128
C_CHUNK = 64
B_WINDOWS = (2, 4, 8, 16)
N_EXPERTS = 8
D_FF = 3584

Z_G, Z_A, Z_DXBC, Z_B, Z_R, Z_K, Z_V, Z_DZ, Z_LORA, Z_DT = (
    0, 4096, 5120, 6144, 6656, 7168, 7680, 8192, 8704, 8960)
Z_WIDTH = 9216

TM_PROJ = 1024
TN_PROJ = 1024
TM_A = 512
TM_B = 512
TM_C = 256
TM_D = 256
TM_MERGE = 512
TM_FFN = 512
TM_MOE = 512
TF_FFN = 1792
TM_ROUTE = 512
TM_DISPATCH = 1024
TM_COMBINE = 256
VMEM_LIMIT = 56 * 1024 * 1024


def _cp(*sem):
    return pltpu.CompilerParams(dimension_semantics=sem, vmem_limit_bytes=VMEM_LIMIT)


def _bdot(a, b):
    return jnp.dot(a.astype(BF16), b.astype(BF16), preferred_element_type=F32)


def _fdot(a, b):
    return jnp.dot(a, b, preferred_element_type=F32, precision=HIGHEST)


def _dot_nt(a, b, precision=None):
    return lax.dot_general(a, b, (((1,), (1,)), ((), ())), preferred_element_type=F32,
                           precision=precision)


def _dot_tn(a, b, precision=None):
    return lax.dot_general(a, b, (((0,), (0,)), ((), ())), preferred_element_type=F32,
                           precision=precision)


def _sigmoid(x):
    return 1.0 / (1.0 + jnp.exp(-x))


def _silu(x):
    return x * _sigmoid(x)


def _softplus(x):
    return jnp.maximum(x, 0.0) + jnp.log(1.0 + jnp.exp(-jnp.abs(x)))


def _rmsnorm(x, g):
    return x * lax.rsqrt(jnp.mean(x * x, axis=-1, keepdims=True) + EPS) * g


def _inproj_body(x_ref, g_ref, w_ref, o_ref, h_ref):
    @pl.when(pl.program_id(1) == 0)
    def _():
        h_ref[...] = _rmsnorm(x_ref[...], g_ref[...]).astype(BF16)

    o_ref[...] = jnp.dot(h_ref[...], w_ref[...], preferred_element_type=F32).astype(o_ref.dtype)


def _in_proj(x2, g, w_all, layer):
    n, d = x2.shape
    zw = w_all.shape[-1]
    tm = min(TM_PROJ, n)
    tn = TN_PROJ
    return pl.pallas_call(
        _inproj_body,
        grid=(n // tm, zw // tn),
        in_specs=[pl.BlockSpec((tm, d), lambda i, j: (i, 0)),
                  pl.BlockSpec((1, d), lambda i, j: (0, 0)),
                  pl.BlockSpec((None, d, tn), lambda i, j: (layer, 0, j))],
        out_specs=pl.BlockSpec((tm, tn), lambda i, j: (i, j)),
        out_shape=jax.ShapeDtypeStruct((n, zw), BF16),
        scratch_shapes=[pltpu.VMEM((tm, d), BF16)],
        compiler_params=_cp("parallel", "arbitrary"),
        name="in_proj",
    )(x2, g, w_all)


def _mixa_body(uv_ref, lng_ref, lnb_ref, ws_ref, bs_ref, o_ref):
    tm = uv_ref.shape[0]
    row = lax.broadcasted_iota(jnp.int32, (A_CHUNK, A_CHUNK), 0)
    col = lax.broadcasted_iota(jnp.int32, (A_CHUNK, A_CHUNK), 1)
    ws = [jnp.where(row >= col, ws_ref[g], 0.0).astype(BF16) for g in range(4)]
    for c in range(tm // A_CHUNK):
        rows = slice(c * A_CHUNK, (c + 1) * A_CHUNK)
        uv = jax.nn.gelu(uv_ref[rows, :].astype(F32))
        u = uv[:, :WIDTH]
        v = uv[:, WIDTH:]
        mu = jnp.mean(v, axis=-1, keepdims=True)
        d = v - mu
        var = jnp.mean(d * d, axis=-1, keepdims=True)
        vn = d * lax.rsqrt(var + EPS) * lng_ref[...] + lnb_ref[...]
        for g in range(4):
            cols = slice(g * 128, (g + 1) * 128)
            sg = jnp.dot(ws[g], vn[:, cols].astype(BF16), preferred_element_type=F32)
            sg = sg + bs_ref[:, g:g + 1]
            o_ref[rows, cols] = (u[:, cols] * sg).astype(o_ref.dtype)


def _mixer_a(z, ln_g, ln_b, w_s, b_s_t, seq):
    n = z.shape[0]
    tm = min(TM_A, seq)
    return pl.pallas_call(
        _mixa_body,
        grid=(n // tm,),
        in_specs=[pl.BlockSpec((tm, 2 * WIDTH), lambda i: (i, Z_A // (2 * WIDTH))),
                  pl.BlockSpec((1, WIDTH), lambda i: (0, 0)),
                  pl.BlockSpec((1, WIDTH), lambda i: (0, 0)),
                  pl.BlockSpec((4, A_CHUNK, A_CHUNK), lambda i: (0, 0, 0)),
                  pl.BlockSpec((A_CHUNK, 4), lambda i: (0, 0))],
        out_specs=pl.BlockSpec((tm, WIDTH), lambda i: (i, 0)),
        out_shape=jax.ShapeDtypeStruct((n, WIDTH), BF16),
        compiler_params=_cp("parallel"),
        name="mixer_a",
    )(z, ln_g, ln_b, w_s, b_s_t)


B_PAD = 16


def _mixb_body(p_ref, w_ref, sc_ref, o_ref, pad_ref):
    t = pl.program_id(1)
    tm = p_ref.shape[0]

    @pl.when(t == 0)
    def _():
        pad_ref[0:B_PAD, :] = jnp.zeros((B_PAD, WIDTH), F32)

    @pl.when(t > 0)
    def _():
        pad_ref[0:B_PAD, :] = pad_ref[tm:tm + B_PAD, :]

    pad_ref[B_PAD:B_PAD + tm, :] = p_ref[...].astype(F32)
    pos = (t * tm + lax.broadcasted_iota(jnp.int32, (tm, 1), 0)).astype(F32)
    for g, win in enumerate(B_WINDOWS):
        cols = slice(g * 128, (g + 1) * 128)
        p_g = pad_ref[B_PAD:B_PAD + tm, cols]
        acc = p_g
        for j in range(1, win):
            acc = acc + pad_ref[B_PAD - j:B_PAD - j + tm, cols]
        pooled = acc / jnp.minimum(pos + 1.0, float(win)) - p_g
        y = _bdot(pooled, w_ref[g]) * sc_ref[:, cols]
        o_ref[:, cols] = y.astype(o_ref.dtype)


def _mixer_b(z, w_pool, scale, batch, seq):
    n = z.shape[0]
    tm = min(TM_B, seq)
    nt = seq // tm
    return pl.pallas_call(
        _mixb_body,
        grid=(batch, nt),
        in_specs=[pl.BlockSpec((tm, WIDTH), lambda b, t: (b * nt + t, Z_B // WIDTH)),
                  pl.BlockSpec((4, 128, 128), lambda b, t: (0, 0, 0)),
                  pl.BlockSpec((1, WIDTH), lambda b, t: (0, 0))],
        out_specs=pl.BlockSpec((tm, WIDTH), lambda b, t: (b * nt + t, 0)),
        out_shape=jax.ShapeDtypeStruct((n, WIDTH), BF16),
        scratch_shapes=[pltpu.VMEM((tm + B_PAD, WIDTH), F32)],
        compiler_params=_cp("parallel", "arbitrary"),
        name="mixer_b",
    )(z, w_pool, scale)


D_PAD = 8


def _mixd_body(dz_ref, xbc_ref, dt_ref, cw_ref, cb_ref, dtb_ref, alog_ref, dsk_ref, ng_ref,
               o_ref, pad_ref, h_ref, y_ref):
    t = pl.program_id(1)
    tm = xbc_ref.shape[0]
    L = D_CHUNK

    @pl.when(t == 0)
    def _():
        pad_ref[0:D_PAD, :] = jnp.zeros((D_PAD, 2 * WIDTH), F32)
        h_ref[...] = jnp.zeros(h_ref.shape, F32)

    @pl.when(t > 0)
    def _():
        pad_ref[0:D_PAD, :] = pad_ref[tm:tm + D_PAD, :]

    pad_ref[D_PAD:D_PAD + tm, :] = xbc_ref[...].astype(F32)

    row = lax.broadcasted_iota(jnp.int32, (L, L), 0)
    col = lax.broadcasted_iota(jnp.int32, (L, L), 1)
    causal = row >= col
    tril = jnp.where(causal, 1.0, 0.0).astype(F32)
    left = col < HEAD
    top = row < HEAD
    a_head = -jnp.exp(alog_ref[...])

    for c in range(tm // L):
        base = D_PAD + c * L
        rows = slice(c * L, (c + 1) * L)
        conv = cb_ref[...]
        for k in range(4):
            conv = conv + cw_ref[k:k + 1, :] * pad_ref[base - 3 + k:base - 3 + k + L, :]
        xbc = _silu(conv)
        dt = _softplus(dt_ref[rows, :].astype(F32) + dtb_ref[...])
        cs = _fdot(tril, dt * a_head)
        cs_t = cs.T
        last = cs[L - 1:L, :]
        e_cs = jnp.exp(cs)
        e_end = jnp.exp(last - cs)
        e_last = jnp.exp(last)
        for g in range(2):
            b_g = xbc[:, WIDTH + g * D_STATE:WIDTH + (g + 1) * D_STATE].astype(BF16)
            c_g = xbc[:, WIDTH + (2 + g) * D_STATE:WIDTH + (3 + g) * D_STATE].astype(BF16)
            cb = _dot_nt(c_g, b_g)
            for j in range(2):
                pair = g * 2 + j
                h0, h1 = 2 * pair, 2 * pair + 1
                cols = slice(pair * 128, (pair + 1) * 128)
                xp = xbc[:, cols]
                xd = xp * jnp.where(left, dt[:, h0:h0 + 1], dt[:, h1:h1 + 1])
                xd_b = xd.astype(BF16)
                ys = []
                for h in (h0, h1):
                    seg = cs[:, h:h + 1] - cs_t[h:h + 1, :]
                    m = jnp.exp(jnp.where(causal, seg, -jnp.inf)) * cb
                    ys.append(jnp.dot(m.astype(BF16), xd_b, preferred_element_type=F32))
                y_diag = jnp.where(left, ys[0], ys[1])
                hp = h_ref[pair]
                y_off = _dot_nt(c_g, hp.astype(BF16))
                y_off = y_off * jnp.where(left, e_cs[:, h0:h0 + 1], e_cs[:, h1:h1 + 1])
                xdd = xd * jnp.where(left, e_end[:, h0:h0 + 1], e_end[:, h1:h1 + 1])
                dec = jnp.where(top, e_last[:, h0:h0 + 1], e_last[:, h1:h1 + 1])
                h_ref[pair] = hp * dec + _dot_tn(xdd.astype(BF16), b_g)
                y = y_diag + y_off + dsk_ref[:, cols] * xp
                y_ref[:, cols] = y * _silu(dz_ref[rows, cols].astype(F32))
        o_ref[rows, :] = _rmsnorm(y_ref[...], ng_ref[...]).astype(o_ref.dtype)


def _mixer_d(z, conv_w, conv_b, dt_bias, a_log, d_skip, norm_g, batch, seq):
    n = z.shape[0]
    tm = min(TM_D, seq)
    nt = seq // tm
    return pl.pallas_call(
        _mixd_body,
        grid=(batch, nt),
        in_specs=[pl.BlockSpec((tm, WIDTH), lambda b, t: (b * nt + t, Z_DZ // WIDTH)),
                  pl.BlockSpec((tm, 2 * WIDTH), lambda b, t: (b * nt + t, Z_DXBC // (2 * WIDTH))),
                  pl.BlockSpec((tm, 128), lambda b, t: (b * nt + t, Z_DT // 128)),
                  pl.BlockSpec((4, 2 * WIDTH), lambda b, t: (0, 0)),
                  pl.BlockSpec((1, 2 * WIDTH), lambda b, t: (0, 0)),
                  pl.BlockSpec((1, 128), lambda b, t: (0, 0)),
                  pl.BlockSpec((1, 128), lambda b, t: (0, 0)),
                  pl.BlockSpec((1, WIDTH), lambda b, t: (0, 0)),
                  pl.BlockSpec((1, WIDTH), lambda b, t: (0, 0))],
        out_specs=pl.BlockSpec((tm, WIDTH), lambda b, t: (b * nt + t, 0)),
        out_shape=jax.ShapeDtypeStruct((n, WIDTH), BF16),
        scratch_shapes=[pltpu.VMEM((tm + D_PAD, 2 * WIDTH), F32),
                        pltpu.VMEM((N_HEADS // 2, 2 * HEAD, D_STATE), F32),
                        pltpu.VMEM((D_CHUNK, WIDTH), F32)],
        compiler_params=_cp("parallel", "arbitrary"),
        name="mixer_d",
    )(z, z, z, conv_w, conv_b, dt_bias, a_log, d_skip, norm_g)


C_PAD = 8


def _head_sum(x, ones_blk):
    hi = x.astype(BF16)
    lo = (x - hi.astype(F32)).astype(BF16)
    return (jnp.dot(hi, ones_blk, preferred_element_type=F32)
            + jnp.dot(lo, ones_blk, preferred_element_type=F32))


def _mixc_body(r_ref, k_ref, v_ref, lo_ref, mr_ref, mk_ref, mv_ref, ml_ref,
               w0_ref, w2_ref, a0_ref, a2_ref, g2_ref, kk_ref, ka_ref, rk_ref, lng_ref, lnb_ref,
               o_ref,
               rpad, kpad, vpad, lpad, st_ref, r_s, k_s, v_s, a_s, b_s, d_s, g_s, o_s):
    t = pl.program_id(1)
    nq, tm = r_ref.shape[0], r_ref.shape[1]
    C = C_CHUNK

    pads = ((r_ref, rpad), (k_ref, kpad), (v_ref, vpad), (lo_ref, lpad))

    @pl.when(t == 0)
    def _():
        for _, pad in pads:
            pad[:, 0:C_PAD, :] = jnp.zeros((nq, C_PAD, pad.shape[2]), F32)
        st_ref[...] = jnp.zeros(st_ref.shape, F32)

    @pl.when(t > 0)
    def _():
        for _, pad in pads:
            pad[:, 0:C_PAD, :] = pad[:, tm:tm + C_PAD, :]

    for src, pad in pads:
        pad[:, C_PAD:C_PAD + tm, :] = src[...].astype(F32)

    hrow = lax.broadcasted_iota(jnp.int32, (WIDTH, WIDTH), 0) // HEAD
    hcol = lax.broadcasted_iota(jnp.int32, (WIDTH, WIDTH), 1) // HEAD
    ones_blk = jnp.where(hrow == hcol, 1.0, 0.0).astype(BF16)

    for q in range(nq):
        def shifted(pad, mix_ref):
            x = pad[q, C_PAD:C_PAD + tm, :]
            return x + (pad[q, C_PAD - 1:C_PAD - 1 + tm, :] - x) * mix_ref[...]

        r = shifted(rpad, mr_ref)
        k = shifted(kpad, mk_ref)
        v = shifted(vpad, mv_ref)
        lo = shifted(lpad, ml_ref)
        w_log = -_softplus(-(w0_ref[...] + _bdot(jnp.tanh(lo), w2_ref[...]))) - 0.5
        a = _sigmoid(a0_ref[...] + _bdot(lo, a2_ref[...]))
        g_s[q] = _bdot(_sigmoid(lo), g2_ref[...])
        kk = k * kk_ref[...]
        kk = kk / jnp.maximum(jnp.sqrt(_head_sum(kk * kk, ones_blk)), 1e-12)
        r_s[q] = r
        k_s[q] = k * (1.0 + (a - 1.0) * ka_ref[...])
        v_s[q] = v
        a_s[q] = kk
        b_s[q] = kk * a
        d_s[q] = -jnp.exp(w_log)

    row = lax.broadcasted_iota(jnp.int32, (C, C), 0)
    col = lax.broadcasted_iota(jnp.int32, (C, C), 1)
    tril = jnp.where(row >= col, 1.0, 0.0).astype(F32)
    prow = lax.broadcasted_iota(jnp.int32, (C, 2 * HEAD), 0)
    pcol = lax.broadcasted_iota(jnp.int32, (C, 2 * HEAD), 1)
    left = pcol < HEAD
    strict = prow > (pcol & (HEAD - 1))
    incl = prow >= (pcol & (HEAD - 1))
    brow = lax.broadcasted_iota(jnp.int32, (2 * C, 2 * C), 0)
    bcol = lax.broadcasted_iota(jnp.int32, (2 * C, 2 * C), 1)
    eye = jnp.where(brow == bcol, 1.0, 0.0).astype(F32)

    def blk(x):
        return jnp.concatenate([jnp.where(left, x, 0.0), jnp.where(left, 0.0, x)], axis=0)

    def chunk(c, carry):
        rows = pl.ds(pl.multiple_of(c * C, C), C)
        n_pairs = N_HEADS // 2
        pairs = range(nq * n_pairs)
        pcs = [slice((i % n_pairs) * 2 * HEAD, (i % n_pairs + 1) * 2 * HEAD) for i in pairs]
        ar, b_k, k_k, v_k, p_end = [], [], [], [], []
        for q in range(nq):
            ld = d_s[q, rows, :]
            cs = _fdot(tril, ld)
            p_in = jnp.exp(cs)
            p_inv = jnp.exp(-cs)
            rt = r_s[q, rows, :] * p_in
            kt = k_s[q, rows, :] * p_inv
            bt = b_s[q, rows, :] * p_inv
            at = a_s[q, rows, :] * jnp.exp(cs - ld)
            vv = v_s[q, rows, :]
            for pc in pcs[:n_pairs]:
                ar.append(jnp.concatenate([at[:, pc], rt[:, pc]], axis=0).astype(BF16))
                b_k.append(blk(bt[:, pc]).astype(BF16))
                k_k.append(blk(kt[:, pc]).astype(BF16))
                v_k.append(blk(vv[:, pc]).astype(BF16))
                p_end.append(p_in[C - 1:C, pc])
        s = [st_ref[p] for p in pairs]
        m_b = [_dot_nt(ar[p], b_k[p]) for p in pairs]
        m_k = [_dot_nt(ar[p], k_k[p]) for p in pairs]
        m_s = [_dot_nt(ar[p], s[p].astype(BF16)) for p in pairs]
        xp = [blk(-jnp.where(strict, m_b[p][:C], 0.0)).astype(BF16) for p in pairs]
        inv = [eye + xp[p].astype(F32) for p in pairs]
        rhs = [m_s[p][:C] + _bdot(jnp.where(strict, m_k[p][:C], 0.0), v_k[p]) for p in pairs]
        for _ in range(5):
            xp = [_bdot(xp[p], xp[p]).astype(BF16) for p in pairs]
            inv = [inv[p] + _bdot(inv[p], xp[p]) for p in pairs]
        u_k = [(-_bdot(inv[p], blk(rhs[p]))).astype(BF16) for p in pairs]
        for p in pairs:
            uv = jnp.concatenate([u_k[p], v_k[p]], axis=0)
            a_rr = jnp.concatenate([jnp.where(incl, m_b[p][C:], 0.0),
                                    jnp.where(incl, m_k[p][C:], 0.0)], axis=1)
            o_s[p // n_pairs, rows, pcs[p]] = m_s[p][C:] + _bdot(a_rr, uv)
            bk = jnp.concatenate([b_k[p], k_k[p]], axis=0)
            st_ref[p] = (s[p] + _dot_tn(uv, bk)) * p_end[p]
        return carry

    lax.fori_loop(0, tm // C, chunk, 0)

    for q in range(nq):
        o = o_s[q]
        mu = _head_sum(o, ones_blk) * (1.0 / HEAD)
        d = o - mu
        var = _head_sum(d * d, ones_blk) * (1.0 / HEAD)
        on = d * lax.rsqrt(var + GN_EPS) * lng_ref[...] + lnb_ref[...]
        bonus = _head_sum(r_s[q] * k_s[q] * rk_ref[...], ones_blk) * v_s[q]
        o_ref[q] = ((on + bonus) * g_s[q]).astype(o_ref.dtype)


C_SEQS = 4


def _mixer_c(z, mix, w0, w2p, a0, a2p, g2p, k_k, k_a, r_k, ln_g, ln_b, batch, seq):
    n = z.shape[0]
    tm = min(TM_C, seq)
    nt = seq // tm
    nq = C_SEQS if batch % C_SEQS == 0 else 1
    lw = 256
    z3 = z.reshape(batch, seq, z.shape[1])
    tok = lambda off, w: pl.BlockSpec((nq, tm, w), lambda b, t: (b, t, off // w))
    par = lambda rows, w: pl.BlockSpec((rows, w), lambda b, t: (0, 0))
    vm = lambda w: pltpu.VMEM((nq, tm, w), F32)
    pad = lambda w: pltpu.VMEM((nq, tm + C_PAD, w), F32)
    out = pl.pallas_call(
        _mixc_body,
        grid=(batch // nq, nt),
        in_specs=[tok(Z_R, WIDTH), tok(Z_K, WIDTH), tok(Z_V, WIDTH), tok(Z_LORA, lw),
                  par(1, WIDTH), par(1, WIDTH), par(1, WIDTH), par(1, lw),
                  par(1, WIDTH), par(lw, WIDTH), par(1, WIDTH), par(lw, WIDTH), par(lw, WIDTH),
                  par(1, WIDTH), par(1, WIDTH), par(1, WIDTH), par(1, WIDTH), par(1, WIDTH)],
        out_specs=pl.BlockSpec((nq, tm, WIDTH), lambda b, t: (b, t, 0)),
        out_shape=jax.ShapeDtypeStruct((batch, seq, WIDTH), BF16),
        scratch_shapes=[pad(WIDTH), pad(WIDTH), pad(WIDTH), pad(lw),
                        pltpu.VMEM((nq * N_HEADS // 2, 2 * HEAD, 2 * HEAD), F32),
                        vm(WIDTH), vm(WIDTH), vm(WIDTH), vm(WIDTH), vm(WIDTH), vm(WIDTH), vm(WIDTH),
                        vm(WIDTH)],
        compiler_params=_cp("parallel", "arbitrary"),
        name="mixer_c",
    )(z3, z3, z3, z3, mix[:, 0:WIDTH], mix[:, WIDTH:2 * WIDTH], mix[:, 2 * WIDTH:3 * WIDTH],
      mix[:, 3 * WIDTH:], w0, w2p, a0, a2p, g2p, k_k, k_a, r_k, ln_g, ln_b)
    return out.reshape(n, WIDTH)


def _merge_body(ya_ref, yb_ref, yc_ref, yd_ref, zg_ref, x_ref, wb_ref, wo_ref, o_ref):
    merged = None
    for kk, y_ref in enumerate((ya_ref, yb_ref, yc_ref, yd_ref)):
        proj = jnp.dot(y_ref[...], wb_ref[kk], preferred_element_type=F32)
        gate = _sigmoid(zg_ref[:, kk * D_MODEL:(kk + 1) * D_MODEL].astype(F32))
        merged = gate * proj if merged is None else merged + gate * proj
    o_ref[...] = x_ref[...] + jnp.dot(merged.astype(BF16), wo_ref[...], preferred_element_type=F32)


def _merge(ys, z, x2, wb_all, wo_all, layer):
    n = x2.shape[0]
    tm = min(TM_MERGE, n)
    yspec = pl.BlockSpec((tm, WIDTH), lambda i: (i, 0))
    return pl.pallas_call(
        _merge_body,
        grid=(n // tm,),
        in_specs=[yspec, yspec, yspec, yspec,
                  pl.BlockSpec((tm, 4 * D_MODEL), lambda i: (i, Z_G // (4 * D_MODEL))),
                  pl.BlockSpec((tm, D_MODEL), lambda i: (i, 0)),
                  pl.BlockSpec((None, 4, WIDTH, D_MODEL), lambda i: (layer, 0, 0, 0)),
                  pl.BlockSpec((None, D_MODEL, D_MODEL), lambda i: (layer, 0, 0))],
        out_specs=pl.BlockSpec((tm, D_MODEL), lambda i: (i, 0)),
        out_shape=jax.ShapeDtypeStruct((n, D_MODEL), F32),
        compiler_params=_cp("parallel"),
        name="merge",
    )(*ys, z, x2, wb_all, wo_all)


ROW_TILE = 8


def _row_tile_chunk(ref, s, rows):
    return ref[pl.ds(s, rows, stride=ROW_TILE), :]


def _ffn_body(te_ref, na_ref, x_ref, g_ref, wg_ref, wu_ref, wd_ref, o_ref, h_ref, acc_ref, *,
              residual, row_tiles):
    i = pl.program_id(0)
    j = pl.program_id(1)
    active = i < na_ref[0]
    tm, d = h_ref.shape

    @pl.when(jnp.logical_and(active, j == 0))
    def _():
        if row_tiles:
            ss = jnp.zeros((tm, 1), F32)
            for s in range(ROW_TILE):
                xs = _row_tile_chunk(x_ref, s, tm)
                ss = ss + jnp.sum(xs * xs, axis=-1, keepdims=True)
            scale = lax.rsqrt(ss * (1.0 / d) + EPS)
            for s in range(ROW_TILE):
                cols = slice(s * 128, (s + 1) * 128)
                h_ref[:, cols] = (_row_tile_chunk(x_ref, s, tm) * scale * g_ref[:, cols]).astype(BF16)
        else:
            h_ref[...] = _rmsnorm(x_ref[...], g_ref[...]).astype(BF16)
        acc_ref[...] = jnp.zeros(acc_ref.shape, F32)

    @pl.when(active)
    def _():
        h = h_ref[...]
        gate = jnp.dot(h, wg_ref[...], preferred_element_type=F32)
        up = jnp.dot(h, wu_ref[...], preferred_element_type=F32)
        act = (_silu(gate) * up).astype(BF16)
        acc_ref[...] += jnp.dot(act, wd_ref[...], preferred_element_type=F32)

    @pl.when(j == pl.num_programs(1) - 1)
    def _():
        @pl.when(active)
        def _():
            if row_tiles:
                for s in range(ROW_TILE):
                    o_ref[pl.ds(s, tm, stride=ROW_TILE), :] = acc_ref[:, s * 128:(s + 1) * 128]
            else:
                o_ref[...] = (x_ref[...] + acc_ref[...]) if residual else acc_ref[...]

        @pl.when(jnp.logical_not(active))
        def _():
            o_ref[...] = jnp.zeros(o_ref.shape, F32)


def _ffn(rows, g, wg_all, wu_all, wd_all, layer_idx, tile_expert, n_active, tm, residual, row_tiles):
    d = wg_all.shape[-2]
    n = rows.shape[0] // ROW_TILE if row_tiles else rows.shape[0]
    d_ff = wg_all.shape[-1]
    tf = TF_FFN
    assert not (row_tiles and residual) and d == ROW_TILE * 128
    io_block = (tm * ROW_TILE, 128) if row_tiles else (tm, d)
    grid_spec = pltpu.PrefetchScalarGridSpec(
        num_scalar_prefetch=2,
        grid=(n // tm, d_ff // tf),
        in_specs=[pl.BlockSpec(io_block, lambda i, j, te, na: (i, 0)),
                  pl.BlockSpec((1, d), lambda i, j, te, na: (0, 0)),
                  pl.BlockSpec((None, None, d, tf), lambda i, j, te, na: (layer_idx, te[i], 0, j)),
                  pl.BlockSpec((None, None, d, tf), lambda i, j, te, na: (layer_idx, te[i], 0, j)),
                  pl.BlockSpec((None, None, tf, d), lambda i, j, te, na: (layer_idx, te[i], j, 0))],
        out_specs=pl.BlockSpec(io_block, lambda i, j, te, na: (i, 0)),
        scratch_shapes=[pltpu.VMEM((tm, d), BF16), pltpu.VMEM((tm, d), F32)],
    )
    return pl.pallas_call(
        functools.partial(_ffn_body, residual=residual, row_tiles=row_tiles),
        grid_spec=grid_spec,
        out_shape=jax.ShapeDtypeStruct(rows.shape, F32),
        compiler_params=_cp("parallel", "arbitrary"),
        name="swiglu_res" if residual else "swiglu_moe",
    )(tile_expert, n_active, rows, g, wg_all, wu_all, wd_all)


def _route_body(x_ref, g_ref, wr_ref, o_ref):
    h = _rmsnorm(x_ref[...], g_ref[...])
    logits = _fdot(h, wr_ref[...])
    lane = lax.broadcasted_iota(jnp.int32, logits.shape, 1)
    lane_f = lane.astype(F32)
    neg = -jnp.inf
    logits = jnp.where(lane < N_EXPERTS, logits, neg)
    m1 = jnp.max(logits, axis=-1, keepdims=True)
    i1 = jnp.min(jnp.where(logits == m1, lane_f, 128.0), axis=-1, keepdims=True)
    rest = jnp.where(lane_f == i1, neg, logits)
    m2 = jnp.max(rest, axis=-1, keepdims=True)
    i2 = jnp.min(jnp.where(rest == m2, lane_f, 128.0), axis=-1, keepdims=True)
    e = jnp.exp(m2 - m1)
    p1 = 1.0 / (1.0 + e)
    p2 = e / (1.0 + e)
    out = jnp.where(lane == 0, i1, 0.0)
    out = jnp.where(lane == 1, i2, out)
    out = jnp.where(lane == 2, p1, out)
    out = jnp.where(lane == 3, p2, out)
    o_ref[...] = out


def _route(x2, g, wr_pad):
    n, d = x2.shape
    tm = min(TM_ROUTE, n)
    return pl.pallas_call(
        _route_body,
        grid=(n // tm,),
        in_specs=[pl.BlockSpec((tm, d), lambda i: (i, 0)),
                  pl.BlockSpec((1, d), lambda i: (0, 0)),
                  pl.BlockSpec((d, 128), lambda i: (0, 0))],
        out_specs=pl.BlockSpec((tm, 128), lambda i: (i, 0)),
        out_shape=jax.ShapeDtypeStruct((n, 128), F32),
        compiler_params=_cp("parallel"),
        name="router",
    )(x2, g, wr_pad)


DMA_UNROLL = 4


def _row_tile(ref, row):
    return ref.at[pl.ds(pl.multiple_of(row * ROW_TILE, ROW_TILE), ROW_TILE)]


def _wait_rows(src, dst, n_rows, sem):
    span = pl.ds(0, n_rows * ROW_TILE)
    pltpu.make_async_copy(src.at[span], dst.at[span], sem).wait()


def _dispatch_body(s1_ref, s2_ref, x_hbm, zero_hbm, o_hbm, sem):
    del zero_hbm
    tm = s1_ref.shape[-1]
    base = pl.program_id(0) * tm

    def issue(it, carry):
        for u in range(DMA_UNROLL):
            r = it * DMA_UNROLL + u
            src = _row_tile(x_hbm, base + r)
            pltpu.make_async_copy(src, _row_tile(o_hbm, s1_ref[0, 0, r]), sem).start()
            pltpu.make_async_copy(src, _row_tile(o_hbm, s2_ref[0, 0, r]), sem).start()
        return carry

    lax.fori_loop(0, tm // DMA_UNROLL, issue, 0)
    _wait_rows(x_hbm, o_hbm, 2 * tm, sem)


def _dispatch(x_tiles, slot1, slot2, n_slots):
    n = slot1.shape[0]
    tm = min(TM_DISPATCH, n)
    sspec = pl.BlockSpec((1, 1, tm), lambda i: (i, 0, 0), memory_space=pltpu.SMEM)
    zeros = jnp.zeros((n_slots * ROW_TILE, 128), F32)
    return pl.pallas_call(
        _dispatch_body,
        grid=(n // tm,),
        in_specs=[sspec, sspec, pl.BlockSpec(memory_space=pl.ANY), pl.BlockSpec(memory_space=pl.ANY)],
        out_specs=pl.BlockSpec(memory_space=pl.ANY),
        out_shape=jax.ShapeDtypeStruct(zeros.shape, F32),
        scratch_shapes=[pltpu.SemaphoreType.DMA(())],
        input_output_aliases={3: 0},
        compiler_params=_cp("arbitrary"),
        name="moe_dispatch",
    )(slot1.reshape(n // tm, 1, tm), slot2.reshape(n // tm, 1, tm), x_tiles, zeros)


def _combine_body(s1_ref, s2_ref, s1n_ref, s2n_ref, y_hbm, x_ref, info_ref, g_ref, o_ref,
                  y1_ref, y2_ref, sems, *, final_norm):
    i = pl.program_id(0)
    tm = x_ref.shape[0]
    buf = lax.rem(i, 2)

    def fetch(i1_ref, i2_ref, b):
        def issue(it, carry):
            for u in range(DMA_UNROLL):
                r = it * DMA_UNROLL + u
                pltpu.make_async_copy(_row_tile(y_hbm, i1_ref[0, 0, r]), _row_tile(y1_ref.at[b], r),
                                      sems.at[0, b]).start(priority=0)
                pltpu.make_async_copy(_row_tile(y_hbm, i2_ref[0, 0, r]), _row_tile(y2_ref.at[b], r),
                                      sems.at[1, b]).start(priority=1)
            return carry

        lax.fori_loop(0, tm // DMA_UNROLL, issue, 0)

    @pl.when(i == 0)
    def _():
        fetch(s1_ref, s2_ref, 0)

    @pl.when(i + 1 < pl.num_programs(0))
    def _():
        fetch(s1n_ref, s2n_ref, 1 - buf)

    _wait_rows(y_hbm, y1_ref.at[buf], tm, sems.at[0, buf])
    _wait_rows(y_hbm, y2_ref.at[buf], tm, sems.at[1, buf])
    p1 = info_ref[:, 2:3]
    p2 = info_ref[:, 3:4]
    for s in range(ROW_TILE):
        cols = slice(s * 128, (s + 1) * 128)
        chunk = pl.ds(s, tm, stride=ROW_TILE)
        o_ref[:, cols] = x_ref[:, cols] + p1 * y1_ref[buf, chunk, :] + p2 * y2_ref[buf, chunk, :]
    if final_norm:
        o_ref[...] = _rmsnorm(o_ref[...], g_ref[...])


def _combine(y_tiles, x2, info, slot1, slot2, g_final, final_norm):
    n, d = x2.shape
    tm = min(TM_COMBINE, n)
    nt = n // tm
    cur = pl.BlockSpec((1, 1, tm), lambda i: (i, 0, 0), memory_space=pltpu.SMEM)
    nxt = pl.BlockSpec((1, 1, tm), lambda i: (jnp.minimum(i + 1, nt - 1), 0, 0), memory_space=pltpu.SMEM)
    s1 = slot1.reshape(nt, 1, tm)
    s2 = slot2.reshape(nt, 1, tm)
    return pl.pallas_call(
        functools.partial(_combine_body, final_norm=final_norm),
        grid=(nt,),
        in_specs=[cur, cur, nxt, nxt,
                  pl.BlockSpec(memory_space=pl.ANY),
                  pl.BlockSpec((tm, d), lambda i: (i, 0)),
                  pl.BlockSpec((tm, 128), lambda i: (i, 0)),
                  pl.BlockSpec((1, d), lambda i: (0, 0))],
        out_specs=pl.BlockSpec((tm, d), lambda i: (i, 0)),
        out_shape=jax.ShapeDtypeStruct((n, d), F32),
        scratch_shapes=[pltpu.VMEM((2, tm * ROW_TILE, 128), F32), pltpu.VMEM((2, tm * ROW_TILE, 128), F32),
                        pltpu.SemaphoreType.DMA((2, 2))],
        compiler_params=_cp("arbitrary"),
        name="moe_combine",
    )(s1, s2, s1, s2, y_tiles, x2, info, g_final)


def _dispatch_plan(info, tm):
    n = info.shape[0]
    experts = info[:, 0:2].astype(jnp.int32)
    flat = experts.reshape(-1)
    onehot = (flat[:, None] == jnp.arange(N_EXPERTS, dtype=jnp.int32)[None, :]).astype(jnp.int32)
    csum = jnp.cumsum(onehot, axis=0)
    rank = jnp.sum(csum * onehot, axis=1) - 1
    counts = csum[-1]
    padded = ((counts + tm - 1) // tm) * tm
    ends = jnp.cumsum(padded)
    starts = ends - padded
    slot = (jnp.sum(starts[None, :] * onehot, axis=1) + rank).astype(jnp.int32)
    n_slots = 2 * n + N_EXPERTS * tm
    tile_start = jnp.arange(n_slots // tm, dtype=jnp.int32) * tm
    tile_expert = jnp.minimum(
        jnp.sum((tile_start[:, None] >= ends[None, :]).astype(jnp.int32), axis=1), N_EXPERTS - 1)
    n_active = (ends[-1] // tm).astype(jnp.int32).reshape(1)
    slots = slot.reshape(n, 2)
    return n_slots, tile_expert.astype(jnp.int32), n_active, slots[:, 0], slots[:, 1]


def _prep_w_in(w_in):
    sl = lambda a, b: w_in[:, :, a:b]
    parts = [sl(4872, 8968), sl(0, 1024), sl(3840, 4864), sl(1024, 1536), sl(1536, 3072),
             sl(3328, 3840), sl(3072, 3328), sl(4864, 4872)]
    w = jnp.concatenate(parts, axis=-1)
    w = jnp.pad(w, ((0, 0), (0, 0), (0, Z_WIDTH - w.shape[-1])))
    return w.astype(BF16)


def kernel(x, norm_mix_g, w_in, a_ln_g, a_ln_b, a_w_s, a_b_s, b_w_pool, b_scale, c_mix, c_w0, c_w2, c_a0, c_a2, c_g2, c_k_k, c_k_a, c_r_k, c_ln_g, c_ln_b, d_conv_w, d_conv_b, d_dt_bias, d_a_log, d_skip, d_norm_g, w_branch, w_out, norm_ffn_g, ffn_w_gate, ffn_w_up, ffn_w_down, moe_router, moe_w_gate, moe_w_up, moe_w_down, norm_final_g):
    batch, seq, d = x.shape
    depth = w_in.shape[0]
    assert depth % 2 == 0, "the final rmsnorm is fused into the last routed layer's combine"
    n = batch * seq
    x2 = x.reshape(n, d)

    w_in_p = _prep_w_in(w_in)
    wb_all = w_branch.astype(BF16)
    wo_all = w_out.astype(BF16)
    ffn_wg = ffn_w_gate.astype(BF16)[:, None]
    ffn_wu = ffn_w_up.astype(BF16)[:, None]
    ffn_wd = ffn_w_down.astype(BF16)[:, None]
    moe_wg = moe_w_gate.astype(BF16)
    moe_wu = moe_w_up.astype(BF16)
    moe_wd = moe_w_down.astype(BF16)

    row = lambda v: v.reshape(1, -1)
    pad_lanes = lambda v: jnp.pad(v, (0, 128 - v.shape[0])).reshape(1, 128)
    dense_tiles = n // min(TM_FFN, n)
    dense_te = jnp.zeros((dense_tiles,), jnp.int32)
    dense_na = jnp.full((1,), dense_tiles, jnp.int32)

    for l in range(depth):
        z = _in_proj(x2, row(norm_mix_g[l]), w_in_p, l)
        y_a = _mixer_a(z, row(a_ln_g[l]), row(a_ln_b[l]), a_w_s[l], a_b_s[l].T, seq)
        y_b = _mixer_b(z, b_w_pool[l], row(b_scale[l]), batch, seq)
        w2p = jnp.pad(c_w2[l], ((0, 192), (0, 0)))
        a2p = jnp.pad(c_a2[l], ((64, 128), (0, 0)))
        g2p = jnp.pad(c_g2[l], ((128, 0), (0, 0)))
        y_c = _mixer_c(z, row(c_mix[l]), row(c_w0[l]), w2p, row(c_a0[l]), a2p, g2p,
                       row(c_k_k[l]), row(c_k_a[l]), row(c_r_k[l]), row(c_ln_g[l]), row(c_ln_b[l]),
                       batch, seq)
        y_d = _mixer_d(z, d_conv_w[l], row(d_conv_b[l]), pad_lanes(d_dt_bias[l]),
                       pad_lanes(d_a_log[l]), row(jnp.repeat(d_skip[l], HEAD)), row(d_norm_g[l]),
                       batch, seq)
        x2 = _merge((y_a, y_b, y_c, y_d), z, x2, wb_all, wo_all, l)

        g_ffn = row(norm_ffn_g[l])
        if l % 2 == 0:
            x2 = _ffn(x2, g_ffn, ffn_wg, ffn_wu, ffn_wd, l // 2, dense_te, dense_na,
                      min(TM_FFN, n), True, False)
        else:
            i = l // 2
            wr_pad = jnp.pad(moe_router[i], ((0, 0), (0, 128 - N_EXPERTS)))
            info = _route(x2, g_ffn, wr_pad)
            n_slots, tile_expert, n_active, slot1, slot2 = _dispatch_plan(info, TM_MOE)
            rows = _dispatch(x2.reshape(n * ROW_TILE, 128), slot1, slot2, n_slots)
            y_tiles = _ffn(rows, g_ffn, moe_wg, moe_wu, moe_wd, i, tile_expert, n_active,
                           TM_MOE, False, True)
            x2 = _combine(y_tiles, x2, info, slot1, slot2, row(norm_final_g), l == depth - 1)
    return x2.reshape(batch, seq, d)
```

```python
import functools

import jax
import jax.numpy as jnp
from jax import lax
from jax.experimental import pallas as pl
from jax.experimental.pallas import tpu as pltpu

F32 = jnp.float32
BF16 = jnp.bfloat16
HIGHEST = lax.Precision.HIGHEST

EPS = 1e-6
GN_EPS = 64e-5
D_MODEL = 1024
N_HEADS = 8
HEAD = 64
WIDTH = 512
A_CHUNK = 128
D_CHUNK = 128
D_STATE = 128
C_CHUNK = 64
B_WINDOWS = (2, 4, 8, 16)
N_EXPERTS = 8
D_FF = 3584

Z_G, Z_A, Z_DXBC, Z_B, Z_R, Z_K, Z_V, Z_DZ, Z_LORA, Z_DT = (
    0, 4096, 5120, 6144, 6656, 7168, 7680, 8192, 8704, 8960)
Z_WIDTH = 9216

TM_PROJ = 1024
TN_PROJ = 1024
TM_A = 512
TM_B = 512
TM_C = 256
TM_D = 256
TM_MERGE = 512
TM_FFN = 512
TM_MOE = 512
TF_FFN = 1792
TM_ROUTE = 512
TM_DISPATCH = 1024
TM_COMBINE = 256
VMEM_LIMIT = 56 * 1024 * 1024


def _cp(*sem):
    return pltpu.CompilerParams(dimension_semantics=sem, vmem_limit_bytes=VMEM_LIMIT)


def _bdot(a, b):
    return jnp.dot(a.astype(BF16), b.astype(BF16), preferred_element_type=F32)


def _fdot(a, b):
    return jnp.dot(a, b, preferred_element_type=F32, precision=HIGHEST)


def _dot_nt(a, b, precision=None):
    return lax.dot_general(a, b, (((1,), (1,)), ((), ())), preferred_element_type=F32,
                           precision=precision)


def _dot_tn(a, b, precision=None):
    return lax.dot_general(a, b, (((0,), (0,)), ((), ())), preferred_element_type=F32,
                           precision=precision)


def _sigmoid(x):
    return 1.0 / (1.0 + jnp.exp(-x))


def _silu(x):
    return x * _sigmoid(x)


def _softplus(x):
    return jnp.maximum(x, 0.0) + jnp.log(1.0 + jnp.exp(-jnp.abs(x)))


def _rmsnorm(x, g):
    return x * lax.rsqrt(jnp.mean(x * x, axis=-1, keepdims=True) + EPS) * g


def _inproj_body(x_ref, g_ref, w_ref, o_ref, h_ref):
    @pl.when(pl.program_id(1) == 0)
    def _():
        h_ref[...] = _rmsnorm(x_ref[...], g_ref[...]).astype(BF16)

    o_ref[...] = jnp.dot(h_ref[...], w_ref[...], preferred_element_type=F32).astype(o_ref.dtype)


def _in_proj(x2, g, w_all, layer):
    n, d = x2.shape
    zw = w_all.shape[-1]
    tm = min(TM_PROJ, n)
    tn = TN_PROJ
    return pl.pallas_call(
        _inproj_body,
        grid=(n // tm, zw // tn),
        in_specs=[pl.BlockSpec((tm, d), lambda i, j: (i, 0)),
                  pl.BlockSpec((1, d), lambda i, j: (0, 0)),
                  pl.BlockSpec((None, d, tn), lambda i, j: (layer, 0, j))],
        out_specs=pl.BlockSpec((tm, tn), lambda i, j: (i, j)),
        out_shape=jax.ShapeDtypeStruct((n, zw), BF16),
        scratch_shapes=[pltpu.VMEM((tm, d), BF16)],
        compiler_params=_cp("parallel", "arbitrary"),
        name="in_proj",
    )(x2, g, w_all)


def _mixa_body(uv_ref, lng_ref, lnb_ref, ws_ref, bs_ref, o_ref):
    tm = uv_ref.shape[0]
    row = lax.broadcasted_iota(jnp.int32, (A_CHUNK, A_CHUNK), 0)
    col = lax.broadcasted_iota(jnp.int32, (A_CHUNK, A_CHUNK), 1)
    ws = [jnp.where(row >= col, ws_ref[g], 0.0).astype(BF16) for g in range(4)]
    for c in range(tm // A_CHUNK):
        rows = slice(c * A_CHUNK, (c + 1) * A_CHUNK)
        uv = jax.nn.gelu(uv_ref[rows, :].astype(F32))
        u = uv[:, :WIDTH]
        v = uv[:, WIDTH:]
        mu = jnp.mean(v, axis=-1, keepdims=True)
        d = v - mu
        var = jnp.mean(d * d, axis=-1, keepdims=True)
        vn = d * lax.rsqrt(var + EPS) * lng_ref[...] + lnb_ref[...]
        for g in range(4):
            cols = slice(g * 128, (g + 1) * 128)
            sg = jnp.dot(ws[g], vn[:, cols].astype(BF16), preferred_element_type=F32)
            sg = sg + bs_ref[:, g:g + 1]
            o_ref[rows, cols] = (u[:, cols] * sg).astype(o_ref.dtype)


def _mixer_a(z, ln_g, ln_b, w_s, b_s_t, seq):
    n = z.shape[0]
    tm = min(TM_A, seq)
    return pl.pallas_call(
        _mixa_body,
        grid=(n // tm,),
        in_specs=[pl.BlockSpec((tm, 2 * WIDTH), lambda i: (i, Z_A // (2 * WIDTH))),
                  pl.BlockSpec((1, WIDTH), lambda i: (0, 0)),
                  pl.BlockSpec((1, WIDTH), lambda i: (0, 0)),
                  pl.BlockSpec((4, A_CHUNK, A_CHUNK), lambda i: (0, 0, 0)),
                  pl.BlockSpec((A_CHUNK, 4), lambda i: (0, 0))],
        out_specs=pl.BlockSpec((tm, WIDTH), lambda i: (i, 0)),
        out_shape=jax.ShapeDtypeStruct((n, WIDTH), BF16),
        compiler_params=_cp("parallel"),
        name="mixer_a",
    )(z, ln_g, ln_b, w_s, b_s_t)


B_PAD = 16


def _mixb_body(p_ref, w_ref, sc_ref, o_ref, pad_ref):
    t = pl.program_id(1)
    tm = p_ref.shape[0]

    @pl.when(t == 0)
    def _():
        pad_ref[0:B_PAD, :] = jnp.zeros((B_PAD, WIDTH), F32)

    @pl.when(t > 0)
    def _():
        pad_ref[0:B_PAD, :] = pad_ref[tm:tm + B_PAD, :]

    pad_ref[B_PAD:B_PAD + tm, :] = p_ref[...].astype(F32)
    pos = (t * tm + lax.broadcasted_iota(jnp.int32, (tm, 1), 0)).astype(F32)
    for g, win in enumerate(B_WINDOWS):
        cols = slice(g * 128, (g + 1) * 128)
        p_g = pad_ref[B_PAD:B_PAD + tm, cols]
        acc = p_g
        for j in range(1, win):
            acc = acc + pad_ref[B_PAD - j:B_PAD - j + tm, cols]
        pooled = acc / jnp.minimum(pos + 1.0, float(win)) - p_g
        y = _bdot(pooled, w_ref[g]) * sc_ref[:, cols]
        o_ref[:, cols] = y.astype(o_ref.dtype)


def _mixer_b(z, w_pool, scale, batch, seq):
    n = z.shape[0]
    tm = min(TM_B, seq)
    nt = seq // tm
    return pl.pallas_call(
        _mixb_body,
        grid=(batch, nt),
        in_specs=[pl.BlockSpec((tm, WIDTH), lambda b, t: (b * nt + t, Z_B // WIDTH)),
                  pl.BlockSpec((4, 128, 128), lambda b, t: (0, 0, 0)),
                  pl.BlockSpec((1, WIDTH), lambda b, t: (0, 0))],
        out_specs=pl.BlockSpec((tm, WIDTH), lambda b, t: (b * nt + t, 0)),
        out_shape=jax.ShapeDtypeStruct((n, WIDTH), BF16),
        scratch_shapes=[pltpu.VMEM((tm + B_PAD, WIDTH), F32)],
        compiler_params=_cp("parallel", "arbitrary"),
        name="mixer_b",
    )(z, w_pool, scale)


D_PAD = 8


def _mixd_body(dz_ref, xbc_ref, dt_ref, cw_ref, cb_ref, dtb_ref, alog_ref, dsk_ref, ng_ref,
               o_ref, pad_ref, h_ref, y_ref):
    t = pl.program_id(1)
    tm = xbc_ref.shape[0]
    L = D_CHUNK

    @pl.when(t == 0)
    def _():
        pad_ref[0:D_PAD, :] = jnp.zeros((D_PAD, 2 * WIDTH), F32)
        h_ref[...] = jnp.zeros(h_ref.shape, F32)

    @pl.when(t > 0)
    def _():
        pad_ref[0:D_PAD, :] = pad_ref[tm:tm + D_PAD, :]

    pad_ref[D_PAD:D_PAD + tm, :] = xbc_ref[...].astype(F32)

    row = lax.broadcasted_iota(jnp.int32, (L, L), 0)
    col = lax.broadcasted_iota(jnp.int32, (L, L), 1)
    causal = row >= col
    tril = jnp.where(causal, 1.0, 0.0).astype(F32)
    left = col < HEAD
    top = row < HEAD
    a_head = -jnp.exp(alog_ref[...])

    for c in range(tm // L):
        base = D_PAD + c * L
        rows = slice(c * L, (c + 1) * L)
        conv = cb_ref[...]
        for k in range(4):
            conv = conv + cw_ref[k:k + 1, :] * pad_ref[base - 3 + k:base - 3 + k + L, :]
        xbc = _silu(conv)
        dt = _softplus(dt_ref[rows, :].astype(F32) + dtb_ref[...])
        cs = _fdot(tril, dt * a_head)
        cs_t = cs.T
        last = cs[L - 1:L, :]
        e_cs = jnp.exp(cs)
        e_end = jnp.exp(last - cs)
        e_last = jnp.exp(last)
        for g in range(2):
            b_g = xbc[:, WIDTH + g * D_STATE:WIDTH + (g + 1) * D_STATE].astype(BF16)
            c_g = xbc[:, WIDTH + (2 + g) * D_STATE:WIDTH + (3 + g) * D_STATE].astype(BF16)
            cb = _dot_nt(c_g, b_g)
            for j in range(2):
                pair = g * 2 + j
                h0, h1 = 2 * pair, 2 * pair + 1
                cols = slice(pair * 128, (pair + 1) * 128)
                xp = xbc[:, cols]
                xd = xp * jnp.where(left, dt[:, h0:h0 + 1], dt[:, h1:h1 + 1])
                xd_b = xd.astype(BF16)
                ys = []
                for h in (h0, h1):
                    seg = cs[:, h:h + 1] - cs_t[h:h + 1, :]
                    m = jnp.exp(jnp.where(causal, seg, -jnp.inf)) * cb
                    ys.append(jnp.dot(m.astype(BF16), xd_b, preferred_element_type=F32))
                y_diag = jnp.where(left, ys[0], ys[1])
                hp = h_ref[pair]
                y_off = _dot_nt(c_g, hp.astype(BF16))
                y_off = y_off * jnp.where(left, e_cs[:, h0:h0 + 1], e_cs[:, h1:h1 + 1])
                xdd = xd * jnp.where(left, e_end[:, h0:h0 + 1], e_end[:, h1:h1 + 1])
                dec = jnp.where(top, e_last[:, h0:h0 + 1], e_last[:, h1:h1 + 1])
                h_ref[pair] = hp * dec + _dot_tn(xdd.astype(BF16), b_g)
                y = y_diag + y_off + dsk_ref[:, cols] * xp
                y_ref[:, cols] = y * _silu(dz_ref[rows, cols].astype(F32))
        o_ref[rows, :] = _rmsnorm(y_ref[...], ng_ref[...]).astype(o_ref.dtype)


def _mixer_d(z, conv_w, conv_b, dt_bias, a_log, d_skip, norm_g, batch, seq):
    n = z.shape[0]
    tm = min(TM_D, seq)
    nt = seq // tm
    return pl.pallas_call(
        _mixd_body,
        grid=(batch, nt),
        in_specs=[pl.BlockSpec((tm, WIDTH), lambda b, t: (b * nt + t, Z_DZ // WIDTH)),
                  pl.BlockSpec((tm, 2 * WIDTH), lambda b, t: (b * nt + t, Z_DXBC // (2 * WIDTH))),
                  pl.BlockSpec((tm, 128), lambda b, t: (b * nt + t, Z_DT // 128)),
                  pl.BlockSpec((4, 2 * WIDTH), lambda b, t: (0, 0)),
                  pl.BlockSpec((1, 2 * WIDTH), lambda b, t: (0, 0)),
                  pl.BlockSpec((1, 128), lambda b, t: (0, 0)),
                  pl.BlockSpec((1, 128), lambda b, t: (0, 0)),
                  pl.BlockSpec((1, WIDTH), lambda b, t: (0, 0)),
                  pl.BlockSpec((1, WIDTH), lambda b, t: (0, 0))],
        out_specs=pl.BlockSpec((tm, WIDTH), lambda b, t: (b * nt + t, 0)),
        out_shape=jax.ShapeDtypeStruct((n, WIDTH), BF16),
        scratch_shapes=[pltpu.VMEM((tm + D_PAD, 2 * WIDTH), F32),
                        pltpu.VMEM((N_HEADS // 2, 2 * HEAD, D_STATE), F32),
                        pltpu.VMEM((D_CHUNK, WIDTH), F32)],
        compiler_params=_cp("parallel", "arbitrary"),
        name="mixer_d",
    )(z, z, z, conv_w, conv_b, dt_bias, a_log, d_skip, norm_g)


C_PAD = 8


def _head_sum(x, ones_blk):
    hi = x.astype(BF16)
    lo = (x - hi.astype(F32)).astype(BF16)
    return (jnp.dot(hi, ones_blk, preferred_element_type=F32)
            + jnp.dot(lo, ones_blk, preferred_element_type=F32))


def _mixc_body(r_ref, k_ref, v_ref, lo_ref, mr_ref, mk_ref, mv_ref, ml_ref,
               w0_ref, w2_ref, a0_ref, a2_ref, g2_ref, kk_ref, ka_ref, rk_ref, lng_ref, lnb_ref,
               o_ref,
               rpad, kpad, vpad, lpad, st_ref, r_s, k_s, v_s, a_s, b_s, d_s, g_s, o_s):
    t = pl.program_id(1)
    nq, tm = r_ref.shape[0], r_ref.shape[1]
    C = C_CHUNK

    pads = ((r_ref, rpad), (k_ref, kpad), (v_ref, vpad), (lo_ref, lpad))

    @pl.when(t == 0)
    def _():
        for _, pad in pads:
            pad[:, 0:C_PAD, :] = jnp.zeros((nq, C_PAD, pad.shape[2]), F32)
        st_ref[...] = jnp.zeros(st_ref.shape, F32)

    @pl.when(t > 0)
    def _():
        for _, pad in pads:
            pad[:, 0:C_PAD, :] = pad[:, tm:tm + C_PAD, :]

    for src, pad in pads:
        pad[:, C_PAD:C_PAD + tm, :] = src[...].astype(F32)

    hrow = lax.broadcasted_iota(jnp.int32, (WIDTH, WIDTH), 0) // HEAD
    hcol = lax.broadcasted_iota(jnp.int32, (WIDTH, WIDTH), 1) // HEAD
    ones_blk = jnp.where(hrow == hcol, 1.0, 0.0).astype(BF16)

    for q in range(nq):
        def shifted(pad, mix_ref):
            x = pad[q, C_PAD:C_PAD + tm, :]
            return x + (pad[q, C_PAD - 1:C_PAD - 1 + tm, :] - x) * mix_ref[...]

        r = shifted(rpad, mr_ref)
        k = shifted(kpad, mk_ref)
        v = shifted(vpad, mv_ref)
        lo = shifted(lpad, ml_ref)
        w_log = -_softplus(-(w0_ref[...] + _bdot(jnp.tanh(lo), w2_ref[...]))) - 0.5
        a = _sigmoid(a0_ref[...] + _bdot(lo, a2_ref[...]))
        g_s[q] = _bdot(_sigmoid(lo), g2_ref[...])
        kk = k * kk_ref[...]
        kk = kk / jnp.maximum(jnp.sqrt(_head_sum(kk * kk, ones_blk)), 1e-12)
        r_s[q] = r
        k_s[q] = k * (1.0 + (a - 1.0) * ka_ref[...])
        v_s[q] = v
        a_s[q] = kk
        b_s[q] = kk * a
        d_s[q] = -jnp.exp(w_log)

    row = lax.broadcasted_iota(jnp.int32, (C, C), 0)
    col = lax.broadcasted_iota(jnp.int32, (C, C), 1)
    tril = jnp.where(row >= col, 1.0, 0.0).astype(F32)
    prow = lax.broadcasted_iota(jnp.int32, (C, 2 * HEAD), 0)
    pcol = lax.broadcasted_iota(jnp.int32, (C, 2 * HEAD), 1)
    left = pcol < HEAD
    strict = prow > (pcol & (HEAD - 1))
    incl = prow >= (pcol & (HEAD - 1))
    brow = lax.broadcasted_iota(jnp.int32, (2 * C, 2 * C), 0)
    bcol = lax.broadcasted_iota(jnp.int32, (2 * C, 2 * C), 1)
    eye = jnp.where(brow == bcol, 1.0, 0.0).astype(F32)

    def blk(x):
        return jnp.concatenate([jnp.where(left, x, 0.0), jnp.where(left, 0.0, x)], axis=0)

    def chunk(c, carry):
        rows = pl.ds(pl.multiple_of(c * C, C), C)
        n_pairs = N_HEADS // 2
        pairs = range(nq * n_pairs)
        pcs = [slice((i % n_pairs) * 2 * HEAD, (i % n_pairs + 1) * 2 * HEAD) for i in pairs]
        ar, b_k, k_k, v_k, p_end = [], [], [], [], []
        for q in range(nq):
            ld = d_s[q, rows, :]
            cs = _fdot(tril, ld)
            p_in = jnp.exp(cs)
            p_inv = jnp.exp(-cs)
            rt = r_s[q, rows, :] * p_in
            kt = k_s[q, rows, :] * p_inv
            bt = b_s[q, rows, :] * p_inv
            at = a_s[q, rows, :] * jnp.exp(cs - ld)
            vv = v_s[q, rows, :]
            for pc in pcs[:n_pairs]:
                ar.append(jnp.concatenate([at[:, pc], rt[:, pc]], axis=0).astype(BF16))
                b_k.append(blk(bt[:, pc]).astype(BF16))
                k_k.append(blk(kt[:, pc]).astype(BF16))
                v_k.append(blk(vv[:, pc]).astype(BF16))
                p_end.append(p_in[C - 1:C, pc])
        s = [st_ref[p] for p in pairs]
        m_b = [_dot_nt(ar[p], b_k[p]) for p in pairs]
        m_k = [_dot_nt(ar[p], k_k[p]) for p in pairs]
        m_s = [_dot_nt(ar[p], s[p].astype(BF16)) for p in pairs]
        xp = [blk(-jnp.where(strict, m_b[p][:C], 0.0)).astype(BF16) for p in pairs]
        inv = [eye + xp[p].astype(F32) for p in pairs]
        rhs = [m_s[p][:C] + _bdot(jnp.where(strict, m_k[p][:C], 0.0), v_k[p]) for p in pairs]
        for _ in range(5):
            xp = [_bdot(xp[p], xp[p]).astype(BF16) for p in pairs]
            inv = [inv[p] + _bdot(inv[p], xp[p]) for p in pairs]
        u_k = [(-_bdot(inv[p], blk(rhs[p]))).astype(BF16) for p in pairs]
        for p in pairs:
            uv = jnp.concatenate([u_k[p], v_k[p]], axis=0)
            a_rr = jnp.concatenate([jnp.where(incl, m_b[p][C:], 0.0),
                                    jnp.where(incl, m_k[p][C:], 0.0)], axis=1)
            o_s[p // n_pairs, rows, pcs[p]] = m_s[p][C:] + _bdot(a_rr, uv)
            bk = jnp.concatenate([b_k[p], k_k[p]], axis=0)
            st_ref[p] = (s[p] + _dot_tn(uv, bk)) * p_end[p]
        return carry

    lax.fori_loop(0, tm // C, chunk, 0)

    for q in range(nq):
        o = o_s[q]
        mu = _head_sum(o, ones_blk) * (1.0 / HEAD)
        d = o - mu
        var = _head_sum(d * d, ones_blk) * (1.0 / HEAD)
        on = d * lax.rsqrt(var + GN_EPS) * lng_ref[...] + lnb_ref[...]
        bonus = _head_sum(r_s[q] * k_s[q] * rk_ref[...], ones_blk) * v_s[q]
        o_ref[q] = ((on + bonus) * g_s[q]).astype(o_ref.dtype)


C_SEQS = 4


def _mixer_c(z, mix, w0, w2p, a0, a2p, g2p, k_k, k_a, r_k, ln_g, ln_b, batch, seq):
    n = z.shape[0]
    tm = min(TM_C, seq)
    nt = seq // tm
    nq = C_SEQS if batch % C_SEQS == 0 else 1
    lw = 256
    z3 = z.reshape(batch, seq, z.shape[1])
    tok = lambda off, w: pl.BlockSpec((nq, tm, w), lambda b, t: (b, t, off // w))
    par = lambda rows, w: pl.BlockSpec((rows, w), lambda b, t: (0, 0))
    vm = lambda w: pltpu.VMEM((nq, tm, w), F32)
    pad = lambda w: pltpu.VMEM((nq, tm + C_PAD, w), F32)
    out = pl.pallas_call(
        _mixc_body,
        grid=(batch // nq, nt),
        in_specs=[tok(Z_R, WIDTH), tok(Z_K, WIDTH), tok(Z_V, WIDTH), tok(Z_LORA, lw),
                  par(1, WIDTH), par(1, WIDTH), par(1, WIDTH), par(1, lw),
                  par(1, WIDTH), par(lw, WIDTH), par(1, WIDTH), par(lw, WIDTH), par(lw, WIDTH),
                  par(1, WIDTH), par(1, WIDTH), par(1, WIDTH), par(1, WIDTH), par(1, WIDTH)],
        out_specs=pl.BlockSpec((nq, tm, WIDTH), lambda b, t: (b, t, 0)),
        out_shape=jax.ShapeDtypeStruct((batch, seq, WIDTH), BF16),
        scratch_shapes=[pad(WIDTH), pad(WIDTH), pad(WIDTH), pad(lw),
                        pltpu.VMEM((nq * N_HEADS // 2, 2 * HEAD, 2 * HEAD), F32),
                        vm(WIDTH), vm(WIDTH), vm(WIDTH), vm(WIDTH), vm(WIDTH), vm(WIDTH), vm(WIDTH),
                        vm(WIDTH)],
        compiler_params=_cp("parallel", "arbitrary"),
        name="mixer_c",
    )(z3, z3, z3, z3, mix[:, 0:WIDTH], mix[:, WIDTH:2 * WIDTH], mix[:, 2 * WIDTH:3 * WIDTH],
      mix[:, 3 * WIDTH:], w0, w2p, a0, a2p, g2p, k_k, k_a, r_k, ln_g, ln_b)
    return out.reshape(n, WIDTH)


def _merge_body(ya_ref, yb_ref, yc_ref, yd_ref, zg_ref, x_ref, wb_ref, wo_ref, o_ref, *tiles_ref):
    tm = x_ref.shape[0]
    merged = None
    for kk, y_ref in enumerate((ya_ref, yb_ref, yc_ref, yd_ref)):
        proj = jnp.dot(y_ref[...], wb_ref[kk], preferred_element_type=F32)
        gate = _sigmoid(zg_ref[:, kk * D_MODEL:(kk + 1) * D_MODEL].astype(F32))
        merged = gate * proj if merged is None else merged + gate * proj
    o_ref[...] = x_ref[...] + jnp.dot(merged.astype(BF16), wo_ref[...], preferred_element_type=F32)
    for t_ref in tiles_ref:
        for s in range(ROW_TILE):
            t_ref[pl.ds(s, tm, stride=ROW_TILE), :] = o_ref[:, s * 128:(s + 1) * 128]


def _merge(ys, z, x2, wb_all, wo_all, layer, with_row_tiles):
    n = x2.shape[0]
    tm = min(TM_MERGE, n)
    yspec = pl.BlockSpec((tm, WIDTH), lambda i: (i, 0))
    out_specs = [pl.BlockSpec((tm, D_MODEL), lambda i: (i, 0))]
    out_shape = [jax.ShapeDtypeStruct((n, D_MODEL), F32)]
    if with_row_tiles:
        out_specs.append(pl.BlockSpec((tm * ROW_TILE, 128), lambda i: (i, 0)))
        out_shape.append(jax.ShapeDtypeStruct((n * ROW_TILE, 128), F32))
    return pl.pallas_call(
        _merge_body,
        grid=(n // tm,),
        in_specs=[yspec, yspec, yspec, yspec,
                  pl.BlockSpec((tm, 4 * D_MODEL), lambda i: (i, Z_G // (4 * D_MODEL))),
                  pl.BlockSpec((tm, D_MODEL), lambda i: (i, 0)),
                  pl.BlockSpec((None, 4, WIDTH, D_MODEL), lambda i: (layer, 0, 0, 0)),
                  pl.BlockSpec((None, D_MODEL, D_MODEL), lambda i: (layer, 0, 0))],
        out_specs=out_specs,
        out_shape=out_shape,
        compiler_params=_cp("parallel"),
        name="merge",
    )(*ys, z, x2, wb_all, wo_all)


ROW_TILE = 8


def _row_tile_chunk(ref, s, rows):
    return ref[pl.ds(s, rows, stride=ROW_TILE), :]


def _ffn_body(te_ref, na_ref, x_ref, g_ref, wg_ref, wu_ref, wd_ref, o_ref, h_ref, acc_ref, *,
              residual, row_tiles):
    i = pl.program_id(0)
    j = pl.program_id(1)
    active = i < na_ref[0]
    tm, d = h_ref.shape

    @pl.when(jnp.logical_and(active, j == 0))
    def _():
        if row_tiles:
            ss = jnp.zeros((tm, 1), F32)
            for s in range(ROW_TILE):
                xs = _row_tile_chunk(x_ref, s, tm)
                ss = ss + jnp.sum(xs * xs, axis=-1, keepdims=True)
            scale = lax.rsqrt(ss * (1.0 / d) + EPS)
            for s in range(ROW_TILE):
                cols = slice(s * 128, (s + 1) * 128)
                h_ref[:, cols] = (_row_tile_chunk(x_ref, s, tm) * scale * g_ref[:, cols]).astype(BF16)
        else:
            h_ref[...] = _rmsnorm(x_ref[...], g_ref[...]).astype(BF16)
        acc_ref[...] = jnp.zeros(acc_ref.shape, F32)

    @pl.when(active)
    def _():
        h = h_ref[...]
        gate = jnp.dot(h, wg_ref[...], preferred_element_type=F32)
        up = jnp.dot(h, wu_ref[...], preferred_element_type=F32)
        act = (_silu(gate) * up).astype(BF16)
        acc_ref[...] += jnp.dot(act, wd_ref[...], preferred_element_type=F32)

    @pl.when(j == pl.num_programs(1) - 1)
    def _():
        @pl.when(active)
        def _():
            if row_tiles:
                for s in range(ROW_TILE):
                    o_ref[pl.ds(s, tm, stride=ROW_TILE), :] = acc_ref[:, s * 128:(s + 1) * 128]
            else:
                o_ref[...] = (x_ref[...] + acc_ref[...]) if residual else acc_ref[...]

        @pl.when(jnp.logical_not(active))
        def _():
            o_ref[...] = jnp.zeros(o_ref.shape, F32)


def _ffn(rows, g, wg_all, wu_all, wd_all, layer_idx, tile_expert, n_active, tm, residual, row_tiles):
    d = wg_all.shape[-2]
    n = rows.shape[0] // ROW_TILE if row_tiles else rows.shape[0]
    d_ff = wg_all.shape[-1]
    tf = TF_FFN
    assert not (row_tiles and residual) and d == ROW_TILE * 128
    io_block = (tm * ROW_TILE, 128) if row_tiles else (tm, d)
    grid_spec = pltpu.PrefetchScalarGridSpec(
        num_scalar_prefetch=2,
        grid=(n // tm, d_ff // tf),
        in_specs=[pl.BlockSpec(io_block, lambda i, j, te, na: (i, 0)),
                  pl.BlockSpec((1, d), lambda i, j, te, na: (0, 0)),
                  pl.BlockSpec((None, None, d, tf), lambda i, j, te, na: (layer_idx, te[i], 0, j)),
                  pl.BlockSpec((None, None, d, tf), lambda i, j, te, na: (layer_idx, te[i], 0, j)),
                  pl.BlockSpec((None, None, tf, d), lambda i, j, te, na: (layer_idx, te[i], j, 0))],
        out_specs=pl.BlockSpec(io_block, lambda i, j, te, na: (i, 0)),
        scratch_shapes=[pltpu.VMEM((tm, d), BF16), pltpu.VMEM((tm, d), F32)],
    )
    return pl.pallas_call(
        functools.partial(_ffn_body, residual=residual, row_tiles=row_tiles),
        grid_spec=grid_spec,
        out_shape=jax.ShapeDtypeStruct(rows.shape, F32),
        compiler_params=_cp("parallel", "arbitrary"),
        name="swiglu_res" if residual else "swiglu_moe",
    )(tile_expert, n_active, rows, g, wg_all, wu_all, wd_all)


def _route_body(x_ref, g_ref, wr_ref, o_ref):
    h = _rmsnorm(x_ref[...], g_ref[...])
    logits = _fdot(h, wr_ref[...])
    lane = lax.broadcasted_iota(jnp.int32, logits.shape, 1)
    lane_f = lane.astype(F32)
    neg = -jnp.inf
    logits = jnp.where(lane < N_EXPERTS, logits, neg)
    m1 = jnp.max(logits, axis=-1, keepdims=True)
    i1 = jnp.min(jnp.where(logits == m1, lane_f, 128.0), axis=-1, keepdims=True)
    rest = jnp.where(lane_f == i1, neg, logits)
    m2 = jnp.max(rest, axis=-1, keepdims=True)
    i2 = jnp.min(jnp.where(rest == m2, lane_f, 128.0), axis=-1, keepdims=True)
    e = jnp.exp(m2 - m1)
    p1 = 1.0 / (1.0 + e)
    p2 = e / (1.0 + e)
    out = jnp.where(lane == 0, i1, 0.0)
    out = jnp.where(lane == 1, i2, out)
    out = jnp.where(lane == 2, p1, out)
    out = jnp.where(lane == 3, p2, out)
    o_ref[...] = out


def _route(x2, g, wr_pad):
    n, d = x2.shape
    tm = min(TM_ROUTE, n)
    return pl.pallas_call(
        _route_body,
        grid=(n // tm,),
        in_specs=[pl.BlockSpec((tm, d), lambda i: (i, 0)),
                  pl.BlockSpec((1, d), lambda i: (0, 0)),
                  pl.BlockSpec((d, 128), lambda i: (0, 0))],
        out_specs=pl.BlockSpec((tm, 128), lambda i: (i, 0)),
        out_shape=jax.ShapeDtypeStruct((n, 128), F32),
        compiler_params=_cp("parallel"),
        name="router",
    )(x2, g, wr_pad)


DMA_UNROLL = 4


def _row_tile(ref, row):
    return ref.at[pl.ds(pl.multiple_of(row * ROW_TILE, ROW_TILE), ROW_TILE)]


def _wait_rows(src, dst, n_rows, sem):
    span = pl.ds(0, n_rows * ROW_TILE)
    pltpu.make_async_copy(src.at[span], dst.at[span], sem).wait()


def _dispatch_body(s1_ref, s2_ref, x_ref, zero_hbm, o_hbm, sems):
    del zero_hbm
    tm = s1_ref.shape[-1]

    def issue(it, carry):
        for u in range(DMA_UNROLL):
            r = it * DMA_UNROLL + u
            src = _row_tile(x_ref, r)
            pltpu.make_async_copy(src, _row_tile(o_hbm, s1_ref[0, 0, r]), sems.at[0]).start(priority=0)
            pltpu.make_async_copy(src, _row_tile(o_hbm, s2_ref[0, 0, r]), sems.at[1]).start(priority=1)
        return carry

    lax.fori_loop(0, tm // DMA_UNROLL, issue, 0)
    _wait_rows(x_ref, o_hbm, tm, sems.at[0])
    _wait_rows(x_ref, o_hbm, tm, sems.at[1])


def _dispatch(x_tiles, slot1, slot2, n_slots):
    n = slot1.shape[0]
    tm = min(TM_DISPATCH, n)
    sspec = pl.BlockSpec((1, 1, tm), lambda i: (i, 0, 0), memory_space=pltpu.SMEM)
    zeros = jnp.zeros((n_slots * ROW_TILE, 128), F32)
    return pl.pallas_call(
        _dispatch_body,
        grid=(n // tm,),
        in_specs=[sspec, sspec, pl.BlockSpec((tm * ROW_TILE, 128), lambda i: (i, 0)),
                  pl.BlockSpec(memory_space=pl.ANY)],
        out_specs=pl.BlockSpec(memory_space=pl.ANY),
        out_shape=jax.ShapeDtypeStruct(zeros.shape, F32),
        scratch_shapes=[pltpu.SemaphoreType.DMA((2,))],
        input_output_aliases={3: 0},
        compiler_params=_cp("arbitrary"),
        name="moe_dispatch",
    )(slot1.reshape(n // tm, 1, tm), slot2.reshape(n // tm, 1, tm), x_tiles, zeros)


def _combine_body(s1_ref, s2_ref, s1n_ref, s2n_ref, y_hbm, x_ref, info_ref, g_ref, o_ref,
                  y1_ref, y2_ref, sems, *, final_norm):
    i = pl.program_id(0)
    tm = x_ref.shape[0]
    buf = lax.rem(i, 2)

    def fetch(i1_ref, i2_ref, b):
        def issue(it, carry):
            for u in range(DMA_UNROLL):
                r = it * DMA_UNROLL + u
                pltpu.make_async_copy(_row_tile(y_hbm, i1_ref[0, 0, r]), _row_tile(y1_ref.at[b], r),
                                      sems.at[0, b]).start(priority=0)
                pltpu.make_async_copy(_row_tile(y_hbm, i2_ref[0, 0, r]), _row_tile(y2_ref.at[b], r),
                                      sems.at[1, b]).start(priority=1)
            return carry

        lax.fori_loop(0, tm // DMA_UNROLL, issue, 0)

    @pl.when(i == 0)
    def _():
        fetch(s1_ref, s2_ref, 0)

    @pl.when(i + 1 < pl.num_programs(0))
    def _():
        fetch(s1n_ref, s2n_ref, 1 - buf)

    _wait_rows(y_hbm, y1_ref.at[buf], tm, sems.at[0, buf])
    _wait_rows(y_hbm, y2_ref.at[buf], tm, sems.at[1, buf])
    p1 = info_ref[:, 2:3]
    p2 = info_ref[:, 3:4]
    for s in range(ROW_TILE):
        cols = slice(s * 128, (s + 1) * 128)
        chunk = pl.ds(s, tm, stride=ROW_TILE)
        o_ref[:, cols] = x_ref[:, cols] + p1 * y1_ref[buf, chunk, :] + p2 * y2_ref[buf, chunk, :]
    if final_norm:
        o_ref[...] = _rmsnorm(o_ref[...], g_ref[...])


def _combine(y_tiles, x2, info, slot1, slot2, g_final, final_norm):
    n, d = x2.shape
    tm = min(TM_COMBINE, n)
    nt = n // tm
    cur = pl.BlockSpec((1, 1, tm), lambda i: (i, 0, 0), memory_space=pltpu.SMEM)
    nxt = pl.BlockSpec((1, 1, tm), lambda i: (jnp.minimum(i + 1, nt - 1), 0, 0), memory_space=pltpu.SMEM)
    s1 = slot1.reshape(nt, 1, tm)
    s2 = slot2.reshape(nt, 1, tm)
    return pl.pallas_call(
        functools.partial(_combine_body, final_norm=final_norm),
        grid=(nt,),
        in_specs=[cur, cur, nxt, nxt,
                  pl.BlockSpec(memory_space=pl.ANY),
                  pl.BlockSpec((tm, d), lambda i: (i, 0)),
                  pl.BlockSpec((tm, 128), lambda i: (i, 0)),
                  pl.BlockSpec((1, d), lambda i: (0, 0))],
        out_specs=pl.BlockSpec((tm, d), lambda i: (i, 0)),
        out_shape=jax.ShapeDtypeStruct((n, d), F32),
        scratch_shapes=[pltpu.VMEM((2, tm * ROW_TILE, 128), F32), pltpu.VMEM((2, tm * ROW_TILE, 128), F32),
                        pltpu.SemaphoreType.DMA((2, 2))],
        compiler_params=_cp("arbitrary"),
        name="moe_combine",
    )(s1, s2, s1, s2, y_tiles, x2, info, g_final)


def _dispatch_plan(info, tm):
    n = info.shape[0]
    experts = info[:, 0:2].astype(jnp.int32)
    flat = experts.reshape(-1)
    onehot = (flat[:, None] == jnp.arange(N_EXPERTS, dtype=jnp.int32)[None, :]).astype(jnp.int32)
    csum = jnp.cumsum(onehot, axis=0)
    rank = jnp.sum(csum * onehot, axis=1) - 1
    counts = csum[-1]
    padded = ((counts + tm - 1) // tm) * tm
    ends = jnp.cumsum(padded)
    starts = ends - padded
    slot = (jnp.sum(starts[None, :] * onehot, axis=1) + rank).astype(jnp.int32)
    n_slots = 2 * n + N_EXPERTS * tm
    tile_start = jnp.arange(n_slots // tm, dtype=jnp.int32) * tm
    tile_expert = jnp.minimum(
        jnp.sum((tile_start[:, None] >= ends[None, :]).astype(jnp.int32), axis=1), N_EXPERTS - 1)
    n_active = (ends[-1] // tm).astype(jnp.int32).reshape(1)
    slots = slot.reshape(n, 2)
    return n_slots, tile_expert.astype(jnp.int32), n_active, slots[:, 0], slots[:, 1]


def _prep_w_in(w_in):
    w = w_in.astype(BF16)
    sl = lambda a, b: w[:, :, a:b]
    parts = [sl(4872, 8968), sl(0, 1024), sl(3840, 4864), sl(1024, 1536), sl(1536, 3072),
             sl(3328, 3840), sl(3072, 3328), sl(4864, 4872),
             jnp.zeros(w.shape[:2] + (Z_WIDTH - w.shape[2],), BF16)]
    return jnp.concatenate(parts, axis=-1)


def kernel(x, norm_mix_g, w_in, a_ln_g, a_ln_b, a_w_s, a_b_s, b_w_pool, b_scale, c_mix, c_w0, c_w2, c_a0, c_a2, c_g2, c_k_k, c_k_a, c_r_k, c_ln_g, c_ln_b, d_conv_w, d_conv_b, d_dt_bias, d_a_log, d_skip, d_norm_g, w_branch, w_out, norm_ffn_g, ffn_w_gate, ffn_w_up, ffn_w_down, moe_router, moe_w_gate, moe_w_up, moe_w_down, norm_final_g):
    batch, seq, d = x.shape
    depth = w_in.shape[0]
    assert depth % 2 == 0, "the final rmsnorm is fused into the last routed layer's combine"
    n = batch * seq
    x2 = x.reshape(n, d)

    w_in_p = _prep_w_in(w_in)
    wb_all = w_branch.astype(BF16)
    wo_all = w_out.astype(BF16)
    ffn_wg = ffn_w_gate.astype(BF16)[:, None]
    ffn_wu = ffn_w_up.astype(BF16)[:, None]
    ffn_wd = ffn_w_down.astype(BF16)[:, None]
    moe_wg = moe_w_gate.astype(BF16)
    moe_wu = moe_w_up.astype(BF16)
    moe_wd = moe_w_down.astype(BF16)

    row = lambda v: v.reshape(1, -1)
    pad_lanes = lambda v: jnp.pad(v, (0, 128 - v.shape[0])).reshape(1, 128)
    dense_tiles = n // min(TM_FFN, n)
    dense_te = jnp.zeros((dense_tiles,), jnp.int32)
    dense_na = jnp.full((1,), dense_tiles, jnp.int32)

    for l in range(depth):
        z = _in_proj(x2, row(norm_mix_g[l]), w_in_p, l)
        y_a = _mixer_a(z, row(a_ln_g[l]), row(a_ln_b[l]), a_w_s[l], a_b_s[l].T, seq)
        y_b = _mixer_b(z, b_w_pool[l], row(b_scale[l]), batch, seq)
        w2p = jnp.pad(c_w2[l], ((0, 192), (0, 0)))
        a2p = jnp.pad(c_a2[l], ((64, 128), (0, 0)))
        g2p = jnp.pad(c_g2[l], ((128, 0), (0, 0)))
        y_c = _mixer_c(z, row(c_mix[l]), row(c_w0[l]), w2p, row(c_a0[l]), a2p, g2p,
                       row(c_k_k[l]), row(c_k_a[l]), row(c_r_k[l]), row(c_ln_g[l]), row(c_ln_b[l]),
                       batch, seq)
        y_d = _mixer_d(z, d_conv_w[l], row(d_conv_b[l]), pad_lanes(d_dt_bias[l]),
                       pad_lanes(d_a_log[l]), row(jnp.repeat(d_skip[l], HEAD)), row(d_norm_g[l]),
                       batch, seq)
        routed = l % 2 == 1
        merged = _merge((y_a, y_b, y_c, y_d), z, x2, wb_all, wo_all, l, routed)
        x2 = merged[0]

        g_ffn = row(norm_ffn_g[l])
        if l % 2 == 0:
            x2 = _ffn(x2, g_ffn, ffn_wg, ffn_wu, ffn_wd, l // 2, dense_te, dense_na,
                      min(TM_FFN, n), True, False)
        else:
            i = l // 2
            wr_pad = jnp.pad(moe_router[i], ((0, 0), (0, 128 - N_EXPERTS)))
            info = _route(x2, g_ffn, wr_pad)
            n_slots, tile_expert, n_active, slot1, slot2 = _dispatch_plan(info, TM_MOE)
            rows = _dispatch(merged[1], slot1, slot2, n_slots)
            y_tiles = _ffn(rows, g_ffn, moe_wg, moe_wu, moe_wd, i, tile_expert, n_active,
                           TM_MOE, False, True)
            x2 = _combine(y_tiles, x2, info, slot1, slot2, row(norm_final_g), l == depth - 1)
    return x2.reshape(batch, seq, d)
```

```python
import functools

import jax
import jax.numpy as jnp
from jax import lax
from jax.experimental import pallas as pl
from jax.experimental.pallas import tpu as pltpu

F32 = jnp.float32
BF16 = jnp.bfloat16
HIGHEST = lax.Precision.HIGHEST

EPS = 1e-6
GN_EPS = 64e-5
D_MODEL = 1024
N_HEADS = 8
HEAD = 64
WIDTH = 512
A_CHUNK = 128
D_CHUNK = 128
D_STATE = 128
C_CHUNK = 64
B_WINDOWS = (2, 4, 8, 16)
N_EXPERTS = 8
D_FF = 3584

Z_G, Z_A, Z_DXBC, Z_B, Z_R, Z_K, Z_V, Z_DZ, Z_LORA, Z_DT = (
    0, 4096, 5120, 6144, 6656, 7168, 7680, 8192, 8704, 8960)
Z_WIDTH = 9216

TM_PROJ = 1024
TN_PROJ = 1024
TM_A = 512
TM_B = 512
TM_C = 256
TM_D = 256
TM_MERGE = 512
TM_FFN = 512
TM_MOE = 512
TF_FFN = 1792
TM_ROUTE = 512
TM_DISPATCH = 1024
TM_COMBINE = 256
VMEM_LIMIT = 56 * 1024 * 1024


def _cp(*sem):
    return pltpu.CompilerParams(dimension_semantics=sem, vmem_limit_bytes=VMEM_LIMIT)


def _bdot(a, b):
    return jnp.dot(a.astype(BF16), b.astype(BF16), preferred_element_type=F32)


def _fdot(a, b):
    return jnp.dot(a, b, preferred_element_type=F32, precision=HIGHEST)


def _dot_nt(a, b, precision=None):
    return lax.dot_general(a, b, (((1,), (1,)), ((), ())), preferred_element_type=F32,
                           precision=precision)


def _dot_tn(a, b, precision=None):
    return lax.dot_general(a, b, (((0,), (0,)), ((), ())), preferred_element_type=F32,
                           precision=precision)


def _sigmoid(x):
    return 1.0 / (1.0 + jnp.exp(-x))


def _silu(x):
    return x * _sigmoid(x)


def _softplus(x):
    return jnp.maximum(x, 0.0) + jnp.log(1.0 + jnp.exp(-jnp.abs(x)))


def _rmsnorm(x, g):
    return x * lax.rsqrt(jnp.mean(x * x, axis=-1, keepdims=True) + EPS) * g


def _inproj_body(x_ref, g_ref, w_ref, o_ref, h_ref):
    @pl.when(pl.program_id(1) == 0)
    def _():
        h_ref[...] = _rmsnorm(x_ref[...], g_ref[...]).astype(BF16)

    o_ref[...] = jnp.dot(h_ref[...], w_ref[...], preferred_element_type=F32).astype(o_ref.dtype)


def _in_proj(x2, g, w_all, layer):
    n, d = x2.shape
    zw = w_all.shape[-1]
    tm = min(TM_PROJ, n)
    tn = TN_PROJ
    return pl.pallas_call(
        _inproj_body,
        grid=(n // tm, zw // tn),
        in_specs=[pl.BlockSpec((tm, d), lambda i, j: (i, 0)),
                  pl.BlockSpec((1, d), lambda i, j: (0, 0)),
                  pl.BlockSpec((None, d, tn), lambda i, j: (layer, 0, j))],
        out_specs=pl.BlockSpec((tm, tn), lambda i, j: (i, j)),
        out_shape=jax.ShapeDtypeStruct((n, zw), BF16),
        scratch_shapes=[pltpu.VMEM((tm, d), BF16)],
        compiler_params=_cp("parallel", "arbitrary"),
        name="in_proj",
    )(x2, g, w_all)


def _mixa_body(uv_ref, lng_ref, lnb_ref, ws_ref, bs_ref, o_ref):
    tm = uv_ref.shape[0]
    row = lax.broadcasted_iota(jnp.int32, (A_CHUNK, A_CHUNK), 0)
    col = lax.broadcasted_iota(jnp.int32, (A_CHUNK, A_CHUNK), 1)
    ws = [jnp.where(row >= col, ws_ref[g], 0.0).astype(BF16) for g in range(4)]
    for c in range(tm // A_CHUNK):
        rows = slice(c * A_CHUNK, (c + 1) * A_CHUNK)
        uv = jax.nn.gelu(uv_ref[rows, :].astype(F32))
        u = uv[:, :WIDTH]
        v = uv[:, WIDTH:]
        mu = jnp.mean(v, axis=-1, keepdims=True)
        d = v - mu
        var = jnp.mean(d * d, axis=-1, keepdims=True)
        vn = d * lax.rsqrt(var + EPS) * lng_ref[...] + lnb_ref[...]
        for g in range(4):
            cols = slice(g * 128, (g + 1) * 128)
            sg = jnp.dot(ws[g], vn[:, cols].astype(BF16), preferred_element_type=F32)
            sg = sg + bs_ref[:, g:g + 1]
            o_ref[rows, cols] = (u[:, cols] * sg).astype(o_ref.dtype)


def _mixer_a(z, ln_g, ln_b, w_s, b_s_t, seq):
    n = z.shape[0]
    tm = min(TM_A, seq)
    return pl.pallas_call(
        _mixa_body,
        grid=(n // tm,),
        in_specs=[pl.BlockSpec((tm, 2 * WIDTH), lambda i: (i, Z_A // (2 * WIDTH))),
                  pl.BlockSpec((1, WIDTH), lambda i: (0, 0)),
                  pl.BlockSpec((1, WIDTH), lambda i: (0, 0)),
                  pl.BlockSpec((4, A_CHUNK, A_CHUNK), lambda i: (0, 0, 0)),
                  pl.BlockSpec((A_CHUNK, 4), lambda i: (0, 0))],
        out_specs=pl.BlockSpec((tm, WIDTH), lambda i: (i, 0)),
        out_shape=jax.ShapeDtypeStruct((n, WIDTH), BF16),
        compiler_params=_cp("parallel"),
        name="mixer_a",
    )(z, ln_g, ln_b, w_s, b_s_t)


B_PAD = 16


def _mixb_body(p_ref, w_ref, sc_ref, o_ref, pad_ref):
    t = pl.program_id(1)
    tm = p_ref.shape[0]

    @pl.when(t == 0)
    def _():
        pad_ref[0:B_PAD, :] = jnp.zeros((B_PAD, WIDTH), F32)

    @pl.when(t > 0)
    def _():
        pad_ref[0:B_PAD, :] = pad_ref[tm:tm + B_PAD, :]

    pad_ref[B_PAD:B_PAD + tm, :] = p_ref[...].astype(F32)
    pos = (t * tm + lax.broadcasted_iota(jnp.int32, (tm, 1), 0)).astype(F32)
    for g, win in enumerate(B_WINDOWS):
        cols = slice(g * 128, (g + 1) * 128)
        p_g = pad_ref[B_PAD:B_PAD + tm, cols]
        acc = p_g
        for j in range(1, win):
            acc = acc + pad_ref[B_PAD - j:B_PAD - j + tm, cols]
        pooled = acc / jnp.minimum(pos + 1.0, float(win)) - p_g
        y = _bdot(pooled, w_ref[g]) * sc_ref[:, cols]
        o_ref[:, cols] = y.astype(o_ref.dtype)


def _mixer_b(z, w_pool, scale, batch, seq):
    n = z.shape[0]
    tm = min(TM_B, seq)
    nt = seq // tm
    return pl.pallas_call(
        _mixb_body,
        grid=(batch, nt),
        in_specs=[pl.BlockSpec((tm, WIDTH), lambda b, t: (b * nt + t, Z_B // WIDTH)),
                  pl.BlockSpec((4, 128, 128), lambda b, t: (0, 0, 0)),
                  pl.BlockSpec((1, WIDTH), lambda b, t: (0, 0))],
        out_specs=pl.BlockSpec((tm, WIDTH), lambda b, t: (b * nt + t, 0)),
        out_shape=jax.ShapeDtypeStruct((n, WIDTH), BF16),
        scratch_shapes=[pltpu.VMEM((tm + B_PAD, WIDTH), F32)],
        compiler_params=_cp("parallel", "arbitrary"),
        name="mixer_b",
    )(z, w_pool, scale)


D_PAD = 8


def _mixd_body(dz_ref, xbc_ref, dt_ref, cw_ref, cb_ref, dtb_ref, alog_ref, dsk_ref, ng_ref,
               o_ref, pad_ref, h_ref, y_ref):
    t = pl.program_id(1)
    tm = xbc_ref.shape[0]
    L = D_CHUNK

    @pl.when(t == 0)
    def _():
        pad_ref[0:D_PAD, :] = jnp.zeros((D_PAD, 2 * WIDTH), F32)
        h_ref[...] = jnp.zeros(h_ref.shape, F32)

    @pl.when(t > 0)
    def _():
        pad_ref[0:D_PAD, :] = pad_ref[tm:tm + D_PAD, :]

    pad_ref[D_PAD:D_PAD + tm, :] = xbc_ref[...].astype(F32)

    row = lax.broadcasted_iota(jnp.int32, (L, L), 0)
    col = lax.broadcasted_iota(jnp.int32, (L, L), 1)
    causal = row >= col
    tril = jnp.where(causal, 1.0, 0.0).astype(F32)
    left = col < HEAD
    top = row < HEAD
    a_head = -jnp.exp(alog_ref[...])

    for c in range(tm // L):
        base = D_PAD + c * L
        rows = slice(c * L, (c + 1) * L)
        conv = cb_ref[...]
        for k in range(4):
            conv = conv + cw_ref[k:k + 1, :] * pad_ref[base - 3 + k:base - 3 + k + L, :]
        xbc = _silu(conv)
        dt = _softplus(dt_ref[rows, :].astype(F32) + dtb_ref[...])
        cs = _fdot(tril, dt * a_head)
        cs_t = cs.T
        last = cs[L - 1:L, :]
        e_cs = jnp.exp(cs)
        e_end = jnp.exp(last - cs)
        e_last = jnp.exp(last)
        for g in range(2):
            b_g = xbc[:, WIDTH + g * D_STATE:WIDTH + (g + 1) * D_STATE].astype(BF16)
            c_g = xbc[:, WIDTH + (2 + g) * D_STATE:WIDTH + (3 + g) * D_STATE].astype(BF16)
            cb = _dot_nt(c_g, b_g)
            for j in range(2):
                pair = g * 2 + j
                h0, h1 = 2 * pair, 2 * pair + 1
                cols = slice(pair * 128, (pair + 1) * 128)
                xp = xbc[:, cols]
                xd = xp * jnp.where(left, dt[:, h0:h0 + 1], dt[:, h1:h1 + 1])
                xd_b = xd.astype(BF16)
                ys = []
                for h in (h0, h1):
                    seg = cs[:, h:h + 1] - cs_t[h:h + 1, :]
                    m = jnp.exp(jnp.where(causal, seg, -jnp.inf)) * cb
                    ys.append(jnp.dot(m.astype(BF16), xd_b, preferred_element_type=F32))
                y_diag = jnp.where(left, ys[0], ys[1])
                hp = h_ref[pair]
                y_off = _dot_nt(c_g, hp.astype(BF16))
                y_off = y_off * jnp.where(left, e_cs[:, h0:h0 + 1], e_cs[:, h1:h1 + 1])
                xdd = xd * jnp.where(left, e_end[:, h0:h0 + 1], e_end[:, h1:h1 + 1])
                dec = jnp.where(top, e_last[:, h0:h0 + 1], e_last[:, h1:h1 + 1])
                h_ref[pair] = hp * dec + _dot_tn(xdd.astype(BF16), b_g)
                y = y_diag + y_off + dsk_ref[:, cols] * xp
                y_ref[:, cols] = y * _silu(dz_ref[rows, cols].astype(F32))
        o_ref[rows, :] = _rmsnorm(y_ref[...], ng_ref[...]).astype(o_ref.dtype)


def _mixer_d(z, conv_w, conv_b, dt_bias, a_log, d_skip, norm_g, batch, seq):
    n = z.shape[0]
    tm = min(TM_D, seq)
    nt = seq // tm
    return pl.pallas_call(
        _mixd_body,
        grid=(batch, nt),
        in_specs=[pl.BlockSpec((tm, WIDTH), lambda b, t: (b * nt + t, Z_DZ // WIDTH)),
                  pl.BlockSpec((tm, 2 * WIDTH), lambda b, t: (b * nt + t, Z_DXBC // (2 * WIDTH))),
                  pl.BlockSpec((tm, 128), lambda b, t: (b * nt + t, Z_DT // 128)),
                  pl.BlockSpec((4, 2 * WIDTH), lambda b, t: (0, 0)),
                  pl.BlockSpec((1, 2 * WIDTH), lambda b, t: (0, 0)),
                  pl.BlockSpec((1, 128), lambda b, t: (0, 0)),
                  pl.BlockSpec((1, 128), lambda b, t: (0, 0)),
                  pl.BlockSpec((1, WIDTH), lambda b, t: (0, 0)),
                  pl.BlockSpec((1, WIDTH), lambda b, t: (0, 0))],
        out_specs=pl.BlockSpec((tm, WIDTH), lambda b, t: (b * nt + t, 0)),
        out_shape=jax.ShapeDtypeStruct((n, WIDTH), BF16),
        scratch_shapes=[pltpu.VMEM((tm + D_PAD, 2 * WIDTH), F32),
                        pltpu.VMEM((N_HEADS // 2, 2 * HEAD, D_STATE), F32),
                        pltpu.VMEM((D_CHUNK, WIDTH), F32)],
        compiler_params=_cp("parallel", "arbitrary"),
        name="mixer_d",
    )(z, z, z, conv_w, conv_b, dt_bias, a_log, d_skip, norm_g)


C_PAD = 8


def _head_sum(x, ones_blk):
    hi = x.astype(BF16)
    lo = (x - hi.astype(F32)).astype(BF16)
    return (jnp.dot(hi, ones_blk, preferred_element_type=F32)
            + jnp.dot(lo, ones_blk, preferred_element_type=F32))


def _mixc_body(r_ref, k_ref, v_ref, lo_ref, mr_ref, mk_ref, mv_ref, ml_ref,
               w0_ref, w2_ref, a0_ref, a2_ref, g2_ref, kk_ref, ka_ref, rk_ref, lng_ref, lnb_ref,
               o_ref,
               rpad, kpad, vpad, lpad, st_ref, r_s, k_s, v_s, a_s, b_s, d_s, g_s, o_s):
    t = pl.program_id(1)
    nq, tm = r_ref.shape[0], r_ref.shape[1]
    C = C_CHUNK

    assert r_ref.dtype == BF16 and lo_ref.dtype == BF16

    @pl.when(t == 0)
    def _():
        for prev in (rpad, kpad, vpad, lpad):
            prev[...] = jnp.zeros(prev.shape, F32)
        st_ref[...] = jnp.zeros(st_ref.shape, F32)

    hrow = lax.broadcasted_iota(jnp.int32, (WIDTH, WIDTH), 0) // HEAD
    hcol = lax.broadcasted_iota(jnp.int32, (WIDTH, WIDTH), 1) // HEAD
    ones_blk = jnp.where(hrow == hcol, 1.0, 0.0).astype(BF16)
    srow = lax.broadcasted_iota(jnp.int32, (tm, tm), 0)
    scol = lax.broadcasted_iota(jnp.int32, (tm, tm), 1)
    shift_mat = jnp.where(srow == scol + 1, 1.0, 0.0).astype(BF16)
    first_row = lax.broadcasted_iota(jnp.int32, (tm, 1), 0) == 0

    for q in range(nq):
        def shifted(src, prev, mix_ref):
            xb = src[q]
            x = xb.astype(F32)
            xs = jnp.dot(shift_mat, xb, preferred_element_type=F32)
            xs = jnp.where(first_row, prev[q, 0:1, :], xs)
            prev[q, 0:1, :] = x[tm - 1:tm, :]
            return x + (xs - x) * mix_ref[...]

        r = shifted(r_ref, rpad, mr_ref)
        k = shifted(k_ref, kpad, mk_ref)
        v = shifted(v_ref, vpad, mv_ref)
        lo = shifted(lo_ref, lpad, ml_ref)
        w_log = -_softplus(-(w0_ref[...] + _bdot(jnp.tanh(lo), w2_ref[...]))) - 0.5
        a = _sigmoid(a0_ref[...] + _bdot(lo, a2_ref[...]))
        g_s[q] = _bdot(_sigmoid(lo), g2_ref[...])
        kk = k * kk_ref[...]
        kk = kk / jnp.maximum(jnp.sqrt(_head_sum(kk * kk, ones_blk)), 1e-12)
        r_s[q] = r
        k_s[q] = k * (1.0 + (a - 1.0) * ka_ref[...])
        v_s[q] = v
        a_s[q] = kk
        b_s[q] = kk * a
        d_s[q] = -jnp.exp(w_log)

    row = lax.broadcasted_iota(jnp.int32, (C, C), 0)
    col = lax.broadcasted_iota(jnp.int32, (C, C), 1)
    tril = jnp.where(row >= col, 1.0, 0.0).astype(F32)
    prow = lax.broadcasted_iota(jnp.int32, (C, 2 * HEAD), 0)
    pcol = lax.broadcasted_iota(jnp.int32, (C, 2 * HEAD), 1)
    left = pcol < HEAD
    strict = prow > (pcol & (HEAD - 1))
    incl = prow >= (pcol & (HEAD - 1))
    brow = lax.broadcasted_iota(jnp.int32, (2 * C, 2 * C), 0)
    bcol = lax.broadcasted_iota(jnp.int32, (2 * C, 2 * C), 1)
    eye = jnp.where(brow == bcol, 1.0, 0.0).astype(F32)

    def blk(x):
        return jnp.concatenate([jnp.where(left, x, 0.0), jnp.where(left, 0.0, x)], axis=0)

    def chunk(c, carry):
        rows = pl.ds(pl.multiple_of(c * C, C), C)
        n_pairs = N_HEADS // 2
        pairs = range(nq * n_pairs)
        pcs = [slice((i % n_pairs) * 2 * HEAD, (i % n_pairs + 1) * 2 * HEAD) for i in pairs]
        ar, b_k, k_k, v_k, p_end = [], [], [], [], []
        for q in range(nq):
            ld = d_s[q, rows, :]
            cs = _fdot(tril, ld)
            p_in = jnp.exp(cs)
            p_inv = jnp.exp(-cs)
            rt = r_s[q, rows, :] * p_in
            kt = k_s[q, rows, :] * p_inv
            bt = b_s[q, rows, :] * p_inv
            at = a_s[q, rows, :] * jnp.exp(cs - ld)
            vv = v_s[q, rows, :]
            for pc in pcs[:n_pairs]:
                ar.append(jnp.concatenate([at[:, pc], rt[:, pc]], axis=0).astype(BF16))
                b_k.append(blk(bt[:, pc]).astype(BF16))
                k_k.append(blk(kt[:, pc]).astype(BF16))
                v_k.append(blk(vv[:, pc]).astype(BF16))
                p_end.append(p_in[C - 1:C, pc])
        s = [st_ref[p] for p in pairs]
        m_b = [_dot_nt(ar[p], b_k[p]) for p in pairs]
        m_k = [_dot_nt(ar[p], k_k[p]) for p in pairs]
        m_s = [_dot_nt(ar[p], s[p].astype(BF16)) for p in pairs]
        xp = [blk(-jnp.where(strict, m_b[p][:C], 0.0)).astype(BF16) for p in pairs]
        inv = [eye + xp[p].astype(F32) for p in pairs]
        rhs = [m_s[p][:C] + _bdot(jnp.where(strict, m_k[p][:C], 0.0), v_k[p]) for p in pairs]
        for _ in range(5):
            xp = [_bdot(xp[p], xp[p]).astype(BF16) for p in pairs]
            inv = [inv[p] + _bdot(inv[p], xp[p]) for p in pairs]
        u_k = [(-_bdot(inv[p], blk(rhs[p]))).astype(BF16) for p in pairs]
        for p in pairs:
            uv = jnp.concatenate([u_k[p], v_k[p]], axis=0)
            a_rr = jnp.concatenate([jnp.where(incl, m_b[p][C:], 0.0),
                                    jnp.where(incl, m_k[p][C:], 0.0)], axis=1)
            o_s[p // n_pairs, rows, pcs[p]] = m_s[p][C:] + _bdot(a_rr, uv)
            bk = jnp.concatenate([b_k[p], k_k[p]], axis=0)
            st_ref[p] = (s[p] + _dot_tn(uv, bk)) * p_end[p]
        return carry

    lax.fori_loop(0, tm // C, chunk, 0)

    for q in range(nq):
        o = o_s[q]
        mu = _head_sum(o, ones_blk) * (1.0 / HEAD)
        d = o - mu
        var = _head_sum(d * d, ones_blk) * (1.0 / HEAD)
        on = d * lax.rsqrt(var + GN_EPS) * lng_ref[...] + lnb_ref[...]
        bonus = _head_sum(r_s[q] * k_s[q] * rk_ref[...], ones_blk) * v_s[q]
        o_ref[q] = ((on + bonus) * g_s[q]).astype(o_ref.dtype)


C_SEQS = 4


def _mixer_c(z, mix, w0, w2p, a0, a2p, g2p, k_k, k_a, r_k, ln_g, ln_b, batch, seq):
    n = z.shape[0]
    tm = min(TM_C, seq)
    nt = seq // tm
    nq = C_SEQS if batch % C_SEQS == 0 else 1
    lw = 256
    z3 = z.reshape(batch, seq, z.shape[1])
    tok = lambda off, w: pl.BlockSpec((nq, tm, w), lambda b, t: (b, t, off // w))
    par = lambda rows, w: pl.BlockSpec((rows, w), lambda b, t: (0, 0))
    vm = lambda w: pltpu.VMEM((nq, tm, w), F32)
    pad = lambda w: pltpu.VMEM((nq, C_PAD, w), F32)
    out = pl.pallas_call(
        _mixc_body,
        grid=(batch // nq, nt),
        in_specs=[tok(Z_R, WIDTH), tok(Z_K, WIDTH), tok(Z_V, WIDTH), tok(Z_LORA, lw),
                  par(1, WIDTH), par(1, WIDTH), par(1, WIDTH), par(1, lw),
                  par(1, WIDTH), par(lw, WIDTH), par(1, WIDTH), par(lw, WIDTH), par(lw, WIDTH),
                  par(1, WIDTH), par(1, WIDTH), par(1, WIDTH), par(1, WIDTH), par(1, WIDTH)],
        out_specs=pl.BlockSpec((nq, tm, WIDTH), lambda b, t: (b, t, 0)),
        out_shape=jax.ShapeDtypeStruct((batch, seq, WIDTH), BF16),
        scratch_shapes=[pad(WIDTH), pad(WIDTH), pad(WIDTH), pad(lw),
                        pltpu.VMEM((nq * N_HEADS // 2, 2 * HEAD, 2 * HEAD), F32),
                        vm(WIDTH), vm(WIDTH), vm(WIDTH), vm(WIDTH), vm(WIDTH), vm(WIDTH), vm(WIDTH),
                        vm(WIDTH)],
        compiler_params=_cp("parallel", "arbitrary"),
        name="mixer_c",
    )(z3, z3, z3, z3, mix[:, 0:WIDTH], mix[:, WIDTH:2 * WIDTH], mix[:, 2 * WIDTH:3 * WIDTH],
      mix[:, 3 * WIDTH:], w0, w2p, a0, a2p, g2p, k_k, k_a, r_k, ln_g, ln_b)
    return out.reshape(n, WIDTH)


def _merge_body(ya_ref, yb_ref, yc_ref, yd_ref, zg_ref, x_ref, wb_ref, wo_ref, o_ref, *tiles_ref):
    tm = x_ref.shape[0]
    merged = None
    for kk, y_ref in enumerate((ya_ref, yb_ref, yc_ref, yd_ref)):
        proj = jnp.dot(y_ref[...], wb_ref[kk], preferred_element_type=F32)
        gate = _sigmoid(zg_ref[:, kk * D_MODEL:(kk + 1) * D_MODEL].astype(F32))
        merged = gate * proj if merged is None else merged + gate * proj
    o_ref[...] = x_ref[...] + jnp.dot(merged.astype(BF16), wo_ref[...], preferred_element_type=F32)
    for t_ref in tiles_ref:
        for s in range(ROW_TILE):
            t_ref[pl.ds(s, tm, stride=ROW_TILE), :] = o_ref[:, s * 128:(s + 1) * 128]


def _merge(ys, z, x2, wb_all, wo_all, layer, with_row_tiles):
    n = x2.shape[0]
    tm = min(TM_MERGE, n)
    yspec = pl.BlockSpec((tm, WIDTH), lambda i: (i, 0))
    out_specs = [pl.BlockSpec((tm, D_MODEL), lambda i: (i, 0))]
    out_shape = [jax.ShapeDtypeStruct((n, D_MODEL), F32)]
    if with_row_tiles:
        out_specs.append(pl.BlockSpec((tm * ROW_TILE, 128), lambda i: (i, 0)))
        out_shape.append(jax.ShapeDtypeStruct((n * ROW_TILE, 128), F32))
    return pl.pallas_call(
        _merge_body,
        grid=(n // tm,),
        in_specs=[yspec, yspec, yspec, yspec,
                  pl.BlockSpec((tm, 4 * D_MODEL), lambda i: (i, Z_G // (4 * D_MODEL))),
                  pl.BlockSpec((tm, D_MODEL), lambda i: (i, 0)),
                  pl.BlockSpec((None, 4, WIDTH, D_MODEL), lambda i: (layer, 0, 0, 0)),
                  pl.BlockSpec((None, D_MODEL, D_MODEL), lambda i: (layer, 0, 0))],
        out_specs=out_specs,
        out_shape=out_shape,
        compiler_params=_cp("parallel"),
        name="merge",
    )(*ys, z, x2, wb_all, wo_all)


ROW_TILE = 8


def _row_tile_chunk(ref, s, rows):
    return ref[pl.ds(s, rows, stride=ROW_TILE), :]


def _ffn_body(te_ref, na_ref, x_ref, g_ref, wg_ref, wu_ref, wd_ref, o_ref, h_ref, acc_ref, *,
              residual, row_tiles):
    i = pl.program_id(0)
    j = pl.program_id(1)
    active = i < na_ref[0]
    tm, d = h_ref.shape

    @pl.when(jnp.logical_and(active, j == 0))
    def _():
        if row_tiles:
            ss = jnp.zeros((tm, 1), F32)
            for s in range(ROW_TILE):
                xs = _row_tile_chunk(x_ref, s, tm)
                ss = ss + jnp.sum(xs * xs, axis=-1, keepdims=True)
            scale = lax.rsqrt(ss * (1.0 / d) + EPS)
            for s in range(ROW_TILE):
                cols = slice(s * 128, (s + 1) * 128)
                h_ref[:, cols] = (_row_tile_chunk(x_ref, s, tm) * scale * g_ref[:, cols]).astype(BF16)
        else:
            h_ref[...] = _rmsnorm(x_ref[...], g_ref[...]).astype(BF16)
        acc_ref[...] = jnp.zeros(acc_ref.shape, F32)

    @pl.when(active)
    def _():
        h = h_ref[...]
        gate = jnp.dot(h, wg_ref[...], preferred_element_type=F32)
        up = jnp.dot(h, wu_ref[...], preferred_element_type=F32)
        act = (_silu(gate) * up).astype(BF16)
        acc_ref[...] += jnp.dot(act, wd_ref[...], preferred_element_type=F32)

    @pl.when(j == pl.num_programs(1) - 1)
    def _():
        @pl.when(active)
        def _():
            if row_tiles:
                for s in range(ROW_TILE):
                    o_ref[pl.ds(s, tm, stride=ROW_TILE), :] = acc_ref[:, s * 128:(s + 1) * 128]
            else:
                o_ref[...] = (x_ref[...] + acc_ref[...]) if residual else acc_ref[...]

        @pl.when(jnp.logical_not(active))
        def _():
            o_ref[...] = jnp.zeros(o_ref.shape, F32)


def _ffn(rows, g, wg_all, wu_all, wd_all, layer_idx, tile_expert, n_active, tm, residual, row_tiles):
    d = wg_all.shape[-2]
    n = rows.shape[0] // ROW_TILE if row_tiles else rows.shape[0]
    d_ff = wg_all.shape[-1]
    tf = TF_FFN
    assert not (row_tiles and residual) and d == ROW_TILE * 128
    io_block = (tm * ROW_TILE, 128) if row_tiles else (tm, d)
    grid_spec = pltpu.PrefetchScalarGridSpec(
        num_scalar_prefetch=2,
        grid=(n // tm, d_ff // tf),
        in_specs=[pl.BlockSpec(io_block, lambda i, j, te, na: (i, 0)),
                  pl.BlockSpec((1, d), lambda i, j, te, na: (0, 0)),
                  pl.BlockSpec((None, None, d, tf), lambda i, j, te, na: (layer_idx, te[i], 0, j)),
                  pl.BlockSpec((None, None, d, tf), lambda i, j, te, na: (layer_idx, te[i], 0, j)),
                  pl.BlockSpec((None, None, tf, d), lambda i, j, te, na: (layer_idx, te[i], j, 0))],
        out_specs=pl.BlockSpec(io_block, lambda i, j, te, na: (i, 0)),
        scratch_shapes=[pltpu.VMEM((tm, d), BF16), pltpu.VMEM((tm, d), F32)],
    )
    return pl.pallas_call(
        functools.partial(_ffn_body, residual=residual, row_tiles=row_tiles),
        grid_spec=grid_spec,
        out_shape=jax.ShapeDtypeStruct(rows.shape, F32),
        compiler_params=_cp("parallel", "arbitrary"),
        name="swiglu_res" if residual else "swiglu_moe",
    )(tile_expert, n_active, rows, g, wg_all, wu_all, wd_all)


def _route_body(x_ref, g_ref, wr_ref, o_ref):
    h = _rmsnorm(x_ref[...], g_ref[...])
    logits = _fdot(h, wr_ref[...])
    lane = lax.broadcasted_iota(jnp.int32, logits.shape, 1)
    lane_f = lane.astype(F32)
    neg = -jnp.inf
    logits = jnp.where(lane < N_EXPERTS, logits, neg)
    m1 = jnp.max(logits, axis=-1, keepdims=True)
    i1 = jnp.min(jnp.where(logits == m1, lane_f, 128.0), axis=-1, keepdims=True)
    rest = jnp.where(lane_f == i1, neg, logits)
    m2 = jnp.max(rest, axis=-1, keepdims=True)
    i2 = jnp.min(jnp.where(rest == m2, lane_f, 128.0), axis=-1, keepdims=True)
    e = jnp.exp(m2 - m1)
    p1 = 1.0 / (1.0 + e)
    p2 = e / (1.0 + e)
    out = jnp.where(lane == 0, i1, 0.0)
    out = jnp.where(lane == 1, i2, out)
    out = jnp.where(lane == 2, p1, out)
    out = jnp.where(lane == 3, p2, out)
    o_ref[...] = out


def _route(x2, g, wr_pad):
    n, d = x2.shape
    tm = min(TM_ROUTE, n)
    return pl.pallas_call(
        _route_body,
        grid=(n // tm,),
        in_specs=[pl.BlockSpec((tm, d), lambda i: (i, 0)),
                  pl.BlockSpec((1, d), lambda i: (0, 0)),
                  pl.BlockSpec((d, 128), lambda i: (0, 0))],
        out_specs=pl.BlockSpec((tm, 128), lambda i: (i, 0)),
        out_shape=jax.ShapeDtypeStruct((n, 128), F32),
        compiler_params=_cp("parallel"),
        name="router",
    )(x2, g, wr_pad)


DMA_UNROLL = 4


def _row_tile(ref, row):
    return ref.at[pl.ds(pl.multiple_of(row * ROW_TILE, ROW_TILE), ROW_TILE)]


def _wait_rows(src, dst, n_rows, sem):
    span = pl.ds(0, n_rows * ROW_TILE)
    pltpu.make_async_copy(src.at[span], dst.at[span], sem).wait()


def _dispatch_body(s1_ref, s2_ref, x_ref, zero_hbm, o_hbm, sems):
    del zero_hbm
    tm = s1_ref.shape[-1]

    def issue(it, carry):
        for u in range(DMA_UNROLL):
            r = it * DMA_UNROLL + u
            src = _row_tile(x_ref, r)
            pltpu.make_async_copy(src, _row_tile(o_hbm, s1_ref[0, 0, r]), sems.at[0]).start(priority=0)
            pltpu.make_async_copy(src, _row_tile(o_hbm, s2_ref[0, 0, r]), sems.at[1]).start(priority=1)
        return carry

    lax.fori_loop(0, tm // DMA_UNROLL, issue, 0)
    _wait_rows(x_ref, o_hbm, tm, sems.at[0])
    _wait_rows(x_ref, o_hbm, tm, sems.at[1])


def _dispatch(x_tiles, slot1, slot2, n_slots):
    n = slot1.shape[0]
    tm = min(TM_DISPATCH, n)
    sspec = pl.BlockSpec((1, 1, tm), lambda i: (i, 0, 0), memory_space=pltpu.SMEM)
    zeros = jnp.zeros((n_slots * ROW_TILE, 128), F32)
    return pl.pallas_call(
        _dispatch_body,
        grid=(n // tm,),
        in_specs=[sspec, sspec, pl.BlockSpec((tm * ROW_TILE, 128), lambda i: (i, 0)),
                  pl.BlockSpec(memory_space=pl.ANY)],
        out_specs=pl.BlockSpec(memory_space=pl.ANY),
        out_shape=jax.ShapeDtypeStruct(zeros.shape, F32),
        scratch_shapes=[pltpu.SemaphoreType.DMA((2,))],
        input_output_aliases={3: 0},
        compiler_params=_cp("arbitrary"),
        name="moe_dispatch",
    )(slot1.reshape(n // tm, 1, tm), slot2.reshape(n // tm, 1, tm), x_tiles, zeros)


def _combine_body(s1_ref, s2_ref, s1n_ref, s2n_ref, y_hbm, x_ref, info_ref, g_ref, o_ref,
                  y1_ref, y2_ref, sems, *, final_norm):
    i = pl.program_id(0)
    tm = x_ref.shape[0]
    buf = lax.rem(i, 2)

    def fetch(i1_ref, i2_ref, b):
        def issue(it, carry):
            for u in range(DMA_UNROLL):
                r = it * DMA_UNROLL + u
                pltpu.make_async_copy(_row_tile(y_hbm, i1_ref[0, 0, r]), _row_tile(y1_ref.at[b], r),
                                      sems.at[0, b]).start(priority=0)
                pltpu.make_async_copy(_row_tile(y_hbm, i2_ref[0, 0, r]), _row_tile(y2_ref.at[b], r),
                                      sems.at[1, b]).start(priority=1)
            return carry

        lax.fori_loop(0, tm // DMA_UNROLL, issue, 0)

    @pl.when(i == 0)
    def _():
        fetch(s1_ref, s2_ref, 0)

    @pl.when(i + 1 < pl.num_programs(0))
    def _():
        fetch(s1n_ref, s2n_ref, 1 - buf)

    _wait_rows(y_hbm, y1_ref.at[buf], tm, sems.at[0, buf])
    _wait_rows(y_hbm, y2_ref.at[buf], tm, sems.at[1, buf])
    p1 = info_ref[:, 2:3]
    p2 = info_ref[:, 3:4]
    for s in range(ROW_TILE):
        cols = slice(s * 128, (s + 1) * 128)
        chunk = pl.ds(s, tm, stride=ROW_TILE)
        o_ref[:, cols] = x_ref[:, cols] + p1 * y1_ref[buf, chunk, :] + p2 * y2_ref[buf, chunk, :]
    if final_norm:
        o_ref[...] = _rmsnorm(o_ref[...], g_ref[...])


def _combine(y_tiles, x2, info, slot1, slot2, g_final, final_norm):
    n, d = x2.shape
    tm = min(TM_COMBINE, n)
    nt = n // tm
    cur = pl.BlockSpec((1, 1, tm), lambda i: (i, 0, 0), memory_space=pltpu.SMEM)
    nxt = pl.BlockSpec((1, 1, tm), lambda i: (jnp.minimum(i + 1, nt - 1), 0, 0), memory_space=pltpu.SMEM)
    s1 = slot1.reshape(nt, 1, tm)
    s2 = slot2.reshape(nt, 1, tm)
    return pl.pallas_call(
        functools.partial(_combine_body, final_norm=final_norm),
        grid=(nt,),
        in_specs=[cur, cur, nxt, nxt,
                  pl.BlockSpec(memory_space=pl.ANY),
                  pl.BlockSpec((tm, d), lambda i: (i, 0)),
                  pl.BlockSpec((tm, 128), lambda i: (i, 0)),
                  pl.BlockSpec((1, d), lambda i: (0, 0))],
        out_specs=pl.BlockSpec((tm, d), lambda i: (i, 0)),
        out_shape=jax.ShapeDtypeStruct((n, d), F32),
        scratch_shapes=[pltpu.VMEM((2, tm * ROW_TILE, 128), F32), pltpu.VMEM((2, tm * ROW_TILE, 128), F32),
                        pltpu.SemaphoreType.DMA((2, 2))],
        compiler_params=_cp("arbitrary"),
        name="moe_combine",
    )(s1, s2, s1, s2, y_tiles, x2, info, g_final)


def _dispatch_plan(info, tm):
    n = info.shape[0]
    experts = info[:, 0:2].astype(jnp.int32)
    flat = experts.reshape(-1)
    onehot = (flat[:, None] == jnp.arange(N_EXPERTS, dtype=jnp.int32)[None, :]).astype(jnp.int32)
    csum = jnp.cumsum(onehot, axis=0)
    rank = jnp.sum(csum * onehot, axis=1) - 1
    counts = csum[-1]
    padded = ((counts + tm - 1) // tm) * tm
    ends = jnp.cumsum(padded)
    starts = ends - padded
    slot = (jnp.sum(starts[None, :] * onehot, axis=1) + rank).astype(jnp.int32)
    n_slots = 2 * n + N_EXPERTS * tm
    tile_start = jnp.arange(n_slots // tm, dtype=jnp.int32) * tm
    tile_expert = jnp.minimum(
        jnp.sum((tile_start[:, None] >= ends[None, :]).astype(jnp.int32), axis=1), N_EXPERTS - 1)
    n_active = (ends[-1] // tm).astype(jnp.int32).reshape(1)
    slots = slot.reshape(n, 2)
    return n_slots, tile_expert.astype(jnp.int32), n_active, slots[:, 0], slots[:, 1]


def _prep_w_in(w_in):
    w = w_in.astype(BF16)
    sl = lambda a, b: w[:, :, a:b]
    parts = [sl(4872, 8968), sl(0, 1024), sl(3840, 4864), sl(1024, 1536), sl(1536, 3072),
             sl(3328, 3840), sl(3072, 3328), sl(4864, 4872),
             jnp.zeros(w.shape[:2] + (Z_WIDTH - w.shape[2],), BF16)]
    return jnp.concatenate(parts, axis=-1)


def kernel(x, norm_mix_g, w_in, a_ln_g, a_ln_b, a_w_s, a_b_s, b_w_pool, b_scale, c_mix, c_w0, c_w2, c_a0, c_a2, c_g2, c_k_k, c_k_a, c_r_k, c_ln_g, c_ln_b, d_conv_w, d_conv_b, d_dt_bias, d_a_log, d_skip, d_norm_g, w_branch, w_out, norm_ffn_g, ffn_w_gate, ffn_w_up, ffn_w_down, moe_router, moe_w_gate, moe_w_up, moe_w_down, norm_final_g):
    batch, seq, d = x.shape
    depth = w_in.shape[0]
    assert depth % 2 == 0, "the final rmsnorm is fused into the last routed layer's combine"
    n = batch * seq
    x2 = x.reshape(n, d)

    w_in_p = _prep_w_in(w_in)
    wb_all = w_branch.astype(BF16)
    wo_all = w_out.astype(BF16)
    ffn_wg = ffn_w_gate.astype(BF16)[:, None]
    ffn_wu = ffn_w_up.astype(BF16)[:, None]
    ffn_wd = ffn_w_down.astype(BF16)[:, None]
    moe_wg = moe_w_gate.astype(BF16)
    moe_wu = moe_w_up.astype(BF16)
    moe_wd = moe_w_down.astype(BF16)

    row = lambda v: v.reshape(1, -1)
    pad_lanes = lambda v: jnp.pad(v, (0, 128 - v.shape[0])).reshape(1, 128)
    dense_tiles = n // min(TM_FFN, n)
    dense_te = jnp.zeros((dense_tiles,), jnp.int32)
    dense_na = jnp.full((1,), dense_tiles, jnp.int32)

    for l in range(depth):
        z = _in_proj(x2, row(norm_mix_g[l]), w_in_p, l)
        y_a = _mixer_a(z, row(a_ln_g[l]), row(a_ln_b[l]), a_w_s[l], a_b_s[l].T, seq)
        y_b = _mixer_b(z, b_w_pool[l], row(b_scale[l]), batch, seq)
        w2p = jnp.pad(c_w2[l], ((0, 192), (0, 0)))
        a2p = jnp.pad(c_a2[l], ((64, 128), (0, 0)))
        g2p = jnp.pad(c_g2[l], ((128, 0), (0, 0)))
        y_c = _mixer_c(z, row(c_mix[l]), row(c_w0[l]), w2p, row(c_a0[l]), a2p, g2p,
                       row(c_k_k[l]), row(c_k_a[l]), row(c_r_k[l]), row(c_ln_g[l]), row(c_ln_b[l]),
                       batch, seq)
        y_d = _mixer_d(z, d_conv_w[l], row(d_conv_b[l]), pad_lanes(d_dt_bias[l]),
                       pad_lanes(d_a_log[l]), row(jnp.repeat(d_skip[l], HEAD)), row(d_norm_g[l]),
                       batch, seq)
        routed = l % 2 == 1
        merged = _merge((y_a, y_b, y_c, y_d), z, x2, wb_all, wo_all, l, routed)
        x2 = merged[0]

        g_ffn = row(norm_ffn_g[l])
        if l % 2 == 0:
            x2 = _ffn(x2, g_ffn, ffn_wg, ffn_wu, ffn_wd, l // 2, dense_te, dense_na,
                      min(TM_FFN, n), True, False)
        else:
            i = l // 2
            wr_pad = jnp.pad(moe_router[i], ((0, 0), (0, 128 - N_EXPERTS)))
            info = _route(x2, g_ffn, wr_pad)
            n_slots, tile_expert, n_active, slot1, slot2 = _dispatch_plan(info, TM_MOE)
            rows = _dispatch(merged[1], slot1, slot2, n_slots)
            y_tiles = _ffn(rows, g_ffn, moe_wg, moe_wu, moe_wd, i, tile_expert, n_active,
                           TM_MOE, False, True)
            x2 = _combine(y_tiles, x2, info, slot1, slot2, row(norm_final_g), l == depth - 1)
    return x2.reshape(batch, seq, d)
```

```python
import functools

import jax
import jax.numpy as jnp
from jax import lax
from jax.experimental import pallas as pl
from jax.experimental.pallas import tpu as pltpu

F32 = jnp.float32
BF16 = jnp.bfloat16
HIGHEST = lax.Precision.HIGHEST

EPS = 1e-6
GN_EPS = 64e-5
D_MODEL = 1024
N_HEADS = 8
HEAD = 64
WIDTH = 512
A_CHUNK = 128
D_CHUNK = 128
D_STATE = 128
C_CHUNK = 64
B_WINDOWS = (2, 4, 8, 16)
N_EXPERTS = 8
D_FF = 3584

Z_G, Z_A, Z_DXBC, Z_B, Z_R, Z_K, Z_V, Z_DZ, Z_LORA, Z_DT = (
    0, 4096, 5120, 6144, 6656, 7168, 7680, 8192, 8704, 8960)
Z_WIDTH = 9216

TM_PROJ = 1024
TN_PROJ = 1024
TM_A = 512
TM_B = 512
TM_C = 256
TM_D = 256
TM_MERGE = 512
TM_FFN = 512
TM_MOE = 512
TF_FFN = 1792
TM_ROUTE = 512
TM_DISPATCH = 1024
TM_COMBINE = 256
VMEM_LIMIT = 56 * 1024 * 1024


def _cp(*sem):
    return pltpu.CompilerParams(dimension_semantics=sem, vmem_limit_bytes=VMEM_LIMIT)


def _bdot(a, b):
    return jnp.dot(a.astype(BF16), b.astype(BF16), preferred_element_type=F32)


def _fdot(a, b):
    return jnp.dot(a, b, preferred_element_type=F32, precision=HIGHEST)


def _dot_nt(a, b, precision=None):
    return lax.dot_general(a, b, (((1,), (1,)), ((), ())), preferred_element_type=F32,
                           precision=precision)


def _dot_tn(a, b, precision=None):
    return lax.dot_general(a, b, (((0,), (0,)), ((), ())), preferred_element_type=F32,
                           precision=precision)


def _sigmoid(x):
    return 1.0 / (1.0 + jnp.exp(-x))


def _silu(x):
    return x * _sigmoid(x)


def _softplus(x):
    return jnp.maximum(x, 0.0) + jnp.log(1.0 + jnp.exp(-jnp.abs(x)))


def _rmsnorm(x, g):
    return x * lax.rsqrt(jnp.mean(x * x, axis=-1, keepdims=True) + EPS) * g


def _inproj_body(x_ref, g_ref, w_ref, o_ref, h_ref):
    @pl.when(pl.program_id(1) == 0)
    def _():
        h_ref[...] = _rmsnorm(x_ref[...], g_ref[...]).astype(BF16)

    o_ref[...] = jnp.dot(h_ref[...], w_ref[...], preferred_element_type=F32).astype(o_ref.dtype)


def _in_proj(x2, g, w_all, layer):
    n, d = x2.shape
    zw = w_all.shape[-1]
    tm = min(TM_PROJ, n)
    tn = TN_PROJ
    return pl.pallas_call(
        _inproj_body,
        grid=(n // tm, zw // tn),
        in_specs=[pl.BlockSpec((tm, d), lambda i, j: (i, 0)),
                  pl.BlockSpec((1, d), lambda i, j: (0, 0)),
                  pl.BlockSpec((None, d, tn), lambda i, j: (layer, 0, j))],
        out_specs=pl.BlockSpec((tm, tn), lambda i, j: (i, j)),
        out_shape=jax.ShapeDtypeStruct((n, zw), BF16),
        scratch_shapes=[pltpu.VMEM((tm, d), BF16)],
        compiler_params=_cp("parallel", "arbitrary"),
        name="in_proj",
    )(x2, g, w_all)


def _mixa_body(uv_ref, lng_ref, lnb_ref, ws_ref, bs_ref, o_ref):
    tm = uv_ref.shape[0]
    row = lax.broadcasted_iota(jnp.int32, (A_CHUNK, A_CHUNK), 0)
    col = lax.broadcasted_iota(jnp.int32, (A_CHUNK, A_CHUNK), 1)
    ws = [jnp.where(row >= col, ws_ref[g], 0.0).astype(BF16) for g in range(4)]
    for c in range(tm // A_CHUNK):
        rows = slice(c * A_CHUNK, (c + 1) * A_CHUNK)
        uv = jax.nn.gelu(uv_ref[rows, :].astype(F32))
        u = uv[:, :WIDTH]
        v = uv[:, WIDTH:]
        mu = jnp.mean(v, axis=-1, keepdims=True)
        d = v - mu
        var = jnp.mean(d * d, axis=-1, keepdims=True)
        vn = d * lax.rsqrt(var + EPS) * lng_ref[...] + lnb_ref[...]
        for g in range(4):
            cols = slice(g * 128, (g + 1) * 128)
            sg = jnp.dot(ws[g], vn[:, cols].astype(BF16), preferred_element_type=F32)
            sg = sg + bs_ref[:, g:g + 1]
            o_ref[rows, cols] = (u[:, cols] * sg).astype(o_ref.dtype)


def _mixer_a(z, ln_g, ln_b, w_s, b_s_t, seq):
    n = z.shape[0]
    tm = min(TM_A, seq)
    return pl.pallas_call(
        _mixa_body,
        grid=(n // tm,),
        in_specs=[pl.BlockSpec((tm, 2 * WIDTH), lambda i: (i, Z_A // (2 * WIDTH))),
                  pl.BlockSpec((1, WIDTH), lambda i: (0, 0)),
                  pl.BlockSpec((1, WIDTH), lambda i: (0, 0)),
                  pl.BlockSpec((4, A_CHUNK, A_CHUNK), lambda i: (0, 0, 0)),
                  pl.BlockSpec((A_CHUNK, 4), lambda i: (0, 0))],
        out_specs=pl.BlockSpec((tm, WIDTH), lambda i: (i, 0)),
        out_shape=jax.ShapeDtypeStruct((n, WIDTH), BF16),
        compiler_params=_cp("parallel"),
        name="mixer_a",
    )(z, ln_g, ln_b, w_s, b_s_t)


B_PAD = 16


def _mixb_body(p_ref, w_ref, sc_ref, o_ref, pad_ref):
    t = pl.program_id(1)
    tm = p_ref.shape[0]

    @pl.when(t == 0)
    def _():
        pad_ref[0:B_PAD, :] = jnp.zeros((B_PAD, WIDTH), F32)

    @pl.when(t > 0)
    def _():
        pad_ref[0:B_PAD, :] = pad_ref[tm:tm + B_PAD, :]

    pad_ref[B_PAD:B_PAD + tm, :] = p_ref[...].astype(F32)
    pos = (t * tm + lax.broadcasted_iota(jnp.int32, (tm, 1), 0)).astype(F32)
    for g, win in enumerate(B_WINDOWS):
        cols = slice(g * 128, (g + 1) * 128)
        p_g = pad_ref[B_PAD:B_PAD + tm, cols]
        acc = p_g
        for j in range(1, win):
            acc = acc + pad_ref[B_PAD - j:B_PAD - j + tm, cols]
        pooled = acc / jnp.minimum(pos + 1.0, float(win)) - p_g
        y = _bdot(pooled, w_ref[g]) * sc_ref[:, cols]
        o_ref[:, cols] = y.astype(o_ref.dtype)


def _mixer_b(z, w_pool, scale, batch, seq):
    n = z.shape[0]
    tm = min(TM_B, seq)
    nt = seq // tm
    return pl.pallas_call(
        _mixb_body,
        grid=(batch, nt),
        in_specs=[pl.BlockSpec((tm, WIDTH), lambda b, t: (b * nt + t, Z_B // WIDTH)),
                  pl.BlockSpec((4, 128, 128), lambda b, t: (0, 0, 0)),
                  pl.BlockSpec((1, WIDTH), lambda b, t: (0, 0))],
        out_specs=pl.BlockSpec((tm, WIDTH), lambda b, t: (b * nt + t, 0)),
        out_shape=jax.ShapeDtypeStruct((n, WIDTH), BF16),
        scratch_shapes=[pltpu.VMEM((tm + B_PAD, WIDTH), F32)],
        compiler_params=_cp("parallel", "arbitrary"),
        name="mixer_b",
    )(z, w_pool, scale)


D_PAD = 8


def _mixd_body(dz_ref, xbc_ref, dt_ref, cw_ref, cb_ref, dtb_ref, alog_ref, dsk_ref, ng_ref,
               o_ref, pad_ref, h_ref, y_ref):
    t = pl.program_id(1)
    tm = xbc_ref.shape[0]
    L = D_CHUNK

    @pl.when(t == 0)
    def _():
        pad_ref[0:D_PAD, :] = jnp.zeros((D_PAD, 2 * WIDTH), F32)
        h_ref[...] = jnp.zeros(h_ref.shape, F32)

    @pl.when(t > 0)
    def _():
        pad_ref[0:D_PAD, :] = pad_ref[tm:tm + D_PAD, :]

    pad_ref[D_PAD:D_PAD + tm, :] = xbc_ref[...].astype(F32)

    row = lax.broadcasted_iota(jnp.int32, (L, L), 0)
    col = lax.broadcasted_iota(jnp.int32, (L, L), 1)
    causal = row >= col
    tril = jnp.where(causal, 1.0, 0.0).astype(F32)
    left = col < HEAD
    top = row < HEAD
    a_head = -jnp.exp(alog_ref[...])

    for c in range(tm // L):
        base = D_PAD + c * L
        rows = slice(c * L, (c + 1) * L)
        conv = cb_ref[...]
        for k in range(4):
            conv = conv + cw_ref[k:k + 1, :] * pad_ref[base - 3 + k:base - 3 + k + L, :]
        xbc = _silu(conv)
        dt = _softplus(dt_ref[rows, :].astype(F32) + dtb_ref[...])
        cs = _fdot(tril, dt * a_head)
        cs_t = cs.T
        last = cs[L - 1:L, :]
        e_cs = jnp.exp(cs)
        e_end = jnp.exp(last - cs)
        e_last = jnp.exp(last)
        for g in range(2):
            b_g = xbc[:, WIDTH + g * D_STATE:WIDTH + (g + 1) * D_STATE].astype(BF16)
            c_g = xbc[:, WIDTH + (2 + g) * D_STATE:WIDTH + (3 + g) * D_STATE].astype(BF16)
            cb = _dot_nt(c_g, b_g)
            for j in range(2):
                pair = g * 2 + j
                h0, h1 = 2 * pair, 2 * pair + 1
                cols = slice(pair * 128, (pair + 1) * 128)
                xp = xbc[:, cols]
                xd = xp * jnp.where(left, dt[:, h0:h0 + 1], dt[:, h1:h1 + 1])
                xd_b = xd.astype(BF16)
                ys = []
                for h in (h0, h1):
                    seg = cs[:, h:h + 1] - cs_t[h:h + 1, :]
                    m = jnp.exp(jnp.where(causal, seg, -jnp.inf)) * cb
                    ys.append(jnp.dot(m.astype(BF16), xd_b, preferred_element_type=F32))
                y_diag = jnp.where(left, ys[0], ys[1])
                hp = h_ref[pair]
                y_off = _dot_nt(c_g, hp.astype(BF16))
                y_off = y_off * jnp.where(left, e_cs[:, h0:h0 + 1], e_cs[:, h1:h1 + 1])
                xdd = xd * jnp.where(left, e_end[:, h0:h0 + 1], e_end[:, h1:h1 + 1])
                dec = jnp.where(top, e_last[:, h0:h0 + 1], e_last[:, h1:h1 + 1])
                h_ref[pair] = hp * dec + _dot_tn(xdd.astype(BF16), b_g)
                y = y_diag + y_off + dsk_ref[:, cols] * xp
                y_ref[:, cols] = y * _silu(dz_ref[rows, cols].astype(F32))
        o_ref[rows, :] = _rmsnorm(y_ref[...], ng_ref[...]).astype(o_ref.dtype)


def _mixer_d(z, conv_w, conv_b, dt_bias, a_log, d_skip, norm_g, batch, seq):
    n = z.shape[0]
    tm = min(TM_D, seq)
    nt = seq // tm
    return pl.pallas_call(
        _mixd_body,
        grid=(batch, nt),
        in_specs=[pl.BlockSpec((tm, WIDTH), lambda b, t: (b * nt + t, Z_DZ // WIDTH)),
                  pl.BlockSpec((tm, 2 * WIDTH), lambda b, t: (b * nt + t, Z_DXBC // (2 * WIDTH))),
                  pl.BlockSpec((tm, 128), lambda b, t: (b * nt + t, Z_DT // 128)),
                  pl.BlockSpec((4, 2 * WIDTH), lambda b, t: (0, 0)),
                  pl.BlockSpec((1, 2 * WIDTH), lambda b, t: (0, 0)),
                  pl.BlockSpec((1, 128), lambda b, t: (0, 0)),
                  pl.BlockSpec((1, 128), lambda b, t: (0, 0)),
                  pl.BlockSpec((1, WIDTH), lambda b, t: (0, 0)),
                  pl.BlockSpec((1, WIDTH), lambda b, t: (0, 0))],
        out_specs=pl.BlockSpec((tm, WIDTH), lambda b, t: (b * nt + t, 0)),
        out_shape=jax.ShapeDtypeStruct((n, WIDTH), BF16),
        scratch_shapes=[pltpu.VMEM((tm + D_PAD, 2 * WIDTH), F32),
                        pltpu.VMEM((N_HEADS // 2, 2 * HEAD, D_STATE), F32),
                        pltpu.VMEM((D_CHUNK, WIDTH), F32)],
        compiler_params=_cp("parallel", "arbitrary"),
        name="mixer_d",
    )(z, z, z, conv_w, conv_b, dt_bias, a_log, d_skip, norm_g)


C_PAD = 8


def _head_sum(x, ones_blk):
    hi = x.astype(BF16)
    lo = (x - hi.astype(F32)).astype(BF16)
    return (jnp.dot(hi, ones_blk, preferred_element_type=F32)
            + jnp.dot(lo, ones_blk, preferred_element_type=F32))


def _mixc_body(r_ref, k_ref, v_ref, lo_ref, mr_ref, mk_ref, mv_ref, ml_ref,
               w0_ref, w2_ref, a0_ref, a2_ref, g2_ref, kk_ref, ka_ref, rk_ref, lng_ref, lnb_ref,
               o_ref,
               rpad, kpad, vpad, lpad, st_ref, r_s, k_s, v_s, a_s, b_s, d_s, g_s, o_s):
    t = pl.program_id(1)
    nq, tm = r_ref.shape[0], r_ref.shape[1]
    C = C_CHUNK

    assert r_ref.dtype == BF16 and lo_ref.dtype == BF16

    @pl.when(t == 0)
    def _():
        for prev in (rpad, kpad, vpad, lpad):
            prev[...] = jnp.zeros(prev.shape, F32)
        st_ref[...] = jnp.zeros(st_ref.shape, F32)

    hrow = lax.broadcasted_iota(jnp.int32, (WIDTH, WIDTH), 0) // HEAD
    hcol = lax.broadcasted_iota(jnp.int32, (WIDTH, WIDTH), 1) // HEAD
    ones_blk = jnp.where(hrow == hcol, 1.0, 0.0).astype(BF16)
    srow = lax.broadcasted_iota(jnp.int32, (tm, tm), 0)
    scol = lax.broadcasted_iota(jnp.int32, (tm, tm), 1)
    shift_mat = jnp.where(srow == scol + 1, 1.0, 0.0).astype(BF16)
    first_row = lax.broadcasted_iota(jnp.int32, (tm, 1), 0) == 0

    for q in range(nq):
        def shifted(src, prev, mix_ref):
            xb = src[q]
            x = xb.astype(F32)
            xs = jnp.dot(shift_mat, xb, preferred_element_type=F32)
            xs = jnp.where(first_row, prev[q, 0:1, :], xs)
            prev[q, 0:1, :] = x[tm - 1:tm, :]
            return x + (xs - x) * mix_ref[...]

        r = shifted(r_ref, rpad, mr_ref)
        k = shifted(k_ref, kpad, mk_ref)
        v = shifted(v_ref, vpad, mv_ref)
        lo = shifted(lo_ref, lpad, ml_ref)
        w_log = -_softplus(-(w0_ref[...] + _bdot(jnp.tanh(lo), w2_ref[...]))) - 0.5
        a = _sigmoid(a0_ref[...] + _bdot(lo, a2_ref[...]))
        g_s[q] = _bdot(_sigmoid(lo), g2_ref[...])
        kk = k * kk_ref[...]
        kk = kk / jnp.maximum(jnp.sqrt(_head_sum(kk * kk, ones_blk)), 1e-12)
        r_s[q] = r
        k_s[q] = k * (1.0 + (a - 1.0) * ka_ref[...])
        v_s[q] = v
        a_s[q] = kk
        b_s[q] = kk * a
        d_s[q] = -jnp.exp(w_log)

    row = lax.broadcasted_iota(jnp.int32, (C, C), 0)
    col = lax.broadcasted_iota(jnp.int32, (C, C), 1)
    tril = jnp.where(row >= col, 1.0, 0.0).astype(F32)
    prow = lax.broadcasted_iota(jnp.int32, (C, 2 * HEAD), 0)
    pcol = lax.broadcasted_iota(jnp.int32, (C, 2 * HEAD), 1)
    left = pcol < HEAD
    strict = prow > (pcol & (HEAD - 1))
    incl = prow >= (pcol & (HEAD - 1))
    brow = lax.broadcasted_iota(jnp.int32, (2 * C, 2 * C), 0)
    bcol = lax.broadcasted_iota(jnp.int32, (2 * C, 2 * C), 1)
    eye = jnp.where(brow == bcol, 1.0, 0.0).astype(F32)

    def blk(x):
        return jnp.concatenate([jnp.where(left, x, 0.0), jnp.where(left, 0.0, x)], axis=0)

    def chunk(c, carry):
        rows = pl.ds(pl.multiple_of(c * C, C), C)
        n_pairs = N_HEADS // 2
        pairs = range(nq * n_pairs)
        pcs = [slice((i % n_pairs) * 2 * HEAD, (i % n_pairs + 1) * 2 * HEAD) for i in pairs]
        ar, b_k, k_k, v_k, p_end = [], [], [], [], []
        for q in range(nq):
            ld = d_s[q, rows, :]
            cs = _fdot(tril, ld)
            p_in = jnp.exp(cs)
            p_inv = jnp.exp(-cs)
            rt = r_s[q, rows, :] * p_in
            kt = k_s[q, rows, :] * p_inv
            bt = b_s[q, rows, :] * p_inv
            at = a_s[q, rows, :] * jnp.exp(cs - ld)
            vv = v_s[q, rows, :]
            for pc in pcs[:n_pairs]:
                ar.append(jnp.concatenate([at[:, pc], rt[:, pc]], axis=0).astype(BF16))
                b_k.append(blk(bt[:, pc]).astype(BF16))
                k_k.append(blk(kt[:, pc]).astype(BF16))
                v_k.append(blk(vv[:, pc]).astype(BF16))
                p_end.append(p_in[C - 1:C, pc])
        s = [st_ref[p] for p in pairs]
        m_b = [_dot_nt(ar[p], b_k[p]) for p in pairs]
        m_k = [_dot_nt(ar[p], k_k[p]) for p in pairs]
        m_s = [_dot_nt(ar[p], s[p].astype(BF16)) for p in pairs]
        xp = [blk(-jnp.where(strict, m_b[p][:C], 0.0)).astype(BF16) for p in pairs]
        inv = [eye + xp[p].astype(F32) for p in pairs]
        rhs = [m_s[p][:C] + _bdot(jnp.where(strict, m_k[p][:C], 0.0), v_k[p]) for p in pairs]
        for _ in range(5):
            xp = [_bdot(xp[p], xp[p]).astype(BF16) for p in pairs]
            inv = [inv[p] + _bdot(inv[p], xp[p]) for p in pairs]
        u_k = [(-_bdot(inv[p], blk(rhs[p]))).astype(BF16) for p in pairs]
        for p in pairs:
            uv = jnp.concatenate([u_k[p], v_k[p]], axis=0)
            a_rr = jnp.concatenate([jnp.where(incl, m_b[p][C:], 0.0),
                                    jnp.where(incl, m_k[p][C:], 0.0)], axis=1)
            o_s[p // n_pairs, rows, pcs[p]] = m_s[p][C:] + _bdot(a_rr, uv)
            bk = jnp.concatenate([b_k[p], k_k[p]], axis=0)
            st_ref[p] = (s[p] + _dot_tn(uv, bk)) * p_end[p]
        return carry

    lax.fori_loop(0, tm // C, chunk, 0)

    for q in range(nq):
        o = o_s[q]
        mu = _head_sum(o, ones_blk) * (1.0 / HEAD)
        d = o - mu
        var = _head_sum(d * d, ones_blk) * (1.0 / HEAD)
        on = d * lax.rsqrt(var + GN_EPS) * lng_ref[...] + lnb_ref[...]
        bonus = _head_sum(r_s[q] * k_s[q] * rk_ref[...], ones_blk) * v_s[q]
        o_ref[q] = ((on + bonus) * g_s[q]).astype(o_ref.dtype)


C_SEQS = 4


def _mixer_c(z, mix, w0, w2p, a0, a2p, g2p, k_k, k_a, r_k, ln_g, ln_b, batch, seq):
    n = z.shape[0]
    tm = min(TM_C, seq)
    nt = seq // tm
    nq = C_SEQS if batch % C_SEQS == 0 else 1
    lw = 256
    z3 = z.reshape(batch, seq, z.shape[1])
    tok = lambda off, w: pl.BlockSpec((nq, tm, w), lambda b, t: (b, t, off // w))
    par = lambda rows, w: pl.BlockSpec((rows, w), lambda b, t: (0, 0))
    vm = lambda w: pltpu.VMEM((nq, tm, w), F32)
    pad = lambda w: pltpu.VMEM((nq, C_PAD, w), F32)
    out = pl.pallas_call(
        _mixc_body,
        grid=(batch // nq, nt),
        in_specs=[tok(Z_R, WIDTH), tok(Z_K, WIDTH), tok(Z_V, WIDTH), tok(Z_LORA, lw),
                  par(1, WIDTH), par(1, WIDTH), par(1, WIDTH), par(1, lw),
                  par(1, WIDTH), par(lw, WIDTH), par(1, WIDTH), par(lw, WIDTH), par(lw, WIDTH),
                  par(1, WIDTH), par(1, WIDTH), par(1, WIDTH), par(1, WIDTH), par(1, WIDTH)],
        out_specs=pl.BlockSpec((nq, tm, WIDTH), lambda b, t: (b, t, 0)),
        out_shape=jax.ShapeDtypeStruct((batch, seq, WIDTH), BF16),
        scratch_shapes=[pad(WIDTH), pad(WIDTH), pad(WIDTH), pad(lw),
                        pltpu.VMEM((nq * N_HEADS // 2, 2 * HEAD, 2 * HEAD), F32),
                        vm(WIDTH), vm(WIDTH), vm(WIDTH), vm(WIDTH), vm(WIDTH), vm(WIDTH), vm(WIDTH),
                        vm(WIDTH)],
        compiler_params=_cp("parallel", "arbitrary"),
        name="mixer_c",
    )(z3, z3, z3, z3, mix[:, 0:WIDTH], mix[:, WIDTH:2 * WIDTH], mix[:, 2 * WIDTH:3 * WIDTH],
      mix[:, 3 * WIDTH:], w0, w2p, a0, a2p, g2p, k_k, k_a, r_k, ln_g, ln_b)
    return out.reshape(n, WIDTH)


def _merge_body(ya_ref, yb_ref, yc_ref, yd_ref, zg_ref, x_ref, wb_ref, wo_ref, o_ref, *tiles_ref):
    tm = x_ref.shape[0]
    merged = None
    for kk, y_ref in enumerate((ya_ref, yb_ref, yc_ref, yd_ref)):
        proj = jnp.dot(y_ref[...], wb_ref[kk], preferred_element_type=F32)
        gate = _sigmoid(zg_ref[:, kk * D_MODEL:(kk + 1) * D_MODEL].astype(F32))
        merged = gate * proj if merged is None else merged + gate * proj
    o_ref[...] = x_ref[...] + jnp.dot(merged.astype(BF16), wo_ref[...], preferred_element_type=F32)
    for t_ref in tiles_ref:
        for s in range(ROW_TILE):
            t_ref[pl.ds(s, tm, stride=ROW_TILE), :] = o_ref[:, s * 128:(s + 1) * 128]


def _merge(ys, z, x2, wb_all, wo_all, layer, with_row_tiles):
    n = x2.shape[0]
    tm = min(TM_MERGE, n)
    yspec = pl.BlockSpec((tm, WIDTH), lambda i: (i, 0))
    out_specs = [pl.BlockSpec((tm, D_MODEL), lambda i: (i, 0))]
    out_shape = [jax.ShapeDtypeStruct((n, D_MODEL), F32)]
    if with_row_tiles:
        out_specs.append(pl.BlockSpec((tm * ROW_TILE, 128), lambda i: (i, 0)))
        out_shape.append(jax.ShapeDtypeStruct((n * ROW_TILE, 128), F32))
    return pl.pallas_call(
        _merge_body,
        grid=(n // tm,),
        in_specs=[yspec, yspec, yspec, yspec,
                  pl.BlockSpec((tm, 4 * D_MODEL), lambda i: (i, Z_G // (4 * D_MODEL))),
                  pl.BlockSpec((tm, D_MODEL), lambda i: (i, 0)),
                  pl.BlockSpec((None, 4, WIDTH, D_MODEL), lambda i: (layer, 0, 0, 0)),
                  pl.BlockSpec((None, D_MODEL, D_MODEL), lambda i: (layer, 0, 0))],
        out_specs=out_specs,
        out_shape=out_shape,
        compiler_params=_cp("parallel"),
        name="merge",
    )(*ys, z, x2, wb_all, wo_all)


ROW_TILE = 8


def _row_tile_chunk(ref, s, rows):
    return ref[pl.ds(s, rows, stride=ROW_TILE), :]


def _ffn_body(te_ref, na_ref, x_ref, g_ref, wg_ref, wu_ref, wd_ref, o_ref, h_ref, acc_ref, *,
              residual, row_tiles):
    i = pl.program_id(0)
    j = pl.program_id(1)
    active = i < na_ref[0]
    tm, d = h_ref.shape

    @pl.when(jnp.logical_and(active, j == 0))
    def _():
        if row_tiles:
            ss = jnp.zeros((tm, 1), F32)
            for s in range(ROW_TILE):
                xs = _row_tile_chunk(x_ref, s, tm)
                ss = ss + jnp.sum(xs * xs, axis=-1, keepdims=True)
            scale = lax.rsqrt(ss * (1.0 / d) + EPS)
            for s in range(ROW_TILE):
                cols = slice(s * 128, (s + 1) * 128)
                h_ref[:, cols] = (_row_tile_chunk(x_ref, s, tm) * scale * g_ref[:, cols]).astype(BF16)
        else:
            h_ref[...] = _rmsnorm(x_ref[...], g_ref[...]).astype(BF16)
        acc_ref[...] = jnp.zeros(acc_ref.shape, F32)

    @pl.when(active)
    def _():
        h = h_ref[...]
        gate = jnp.dot(h, wg_ref[...], preferred_element_type=F32)
        up = jnp.dot(h, wu_ref[...], preferred_element_type=F32)
        act = (_silu(gate) * up).astype(BF16)
        acc_ref[...] += jnp.dot(act, wd_ref[...], preferred_element_type=F32)

    @pl.when(j == pl.num_programs(1) - 1)
    def _():
        @pl.when(active)
        def _():
            if row_tiles:
                for s in range(ROW_TILE):
                    o_ref[pl.ds(s, tm, stride=ROW_TILE), :] = acc_ref[:, s * 128:(s + 1) * 128]
            else:
                o_ref[...] = (x_ref[...] + acc_ref[...]) if residual else acc_ref[...]

        @pl.when(jnp.logical_not(active))
        def _():
            o_ref[...] = jnp.zeros(o_ref.shape, F32)


def _ffn(rows, g, wg_all, wu_all, wd_all, layer_idx, tile_expert, n_active, tm, residual, row_tiles):
    d = wg_all.shape[-2]
    n = rows.shape[0] // ROW_TILE if row_tiles else rows.shape[0]
    d_ff = wg_all.shape[-1]
    tf = TF_FFN
    assert not (row_tiles and residual) and d == ROW_TILE * 128
    io_block = (tm * ROW_TILE, 128) if row_tiles else (tm, d)
    last_j = d_ff // tf - 1

    def wj(i, j, na):
        return jnp.where(i < na[0], j, last_j)

    grid_spec = pltpu.PrefetchScalarGridSpec(
        num_scalar_prefetch=2,
        grid=(n // tm, d_ff // tf),
        in_specs=[pl.BlockSpec(io_block, lambda i, j, te, na: (i, 0)),
                  pl.BlockSpec((1, d), lambda i, j, te, na: (0, 0)),
                  pl.BlockSpec((None, None, d, tf), lambda i, j, te, na: (layer_idx, te[i], 0, wj(i, j, na))),
                  pl.BlockSpec((None, None, d, tf), lambda i, j, te, na: (layer_idx, te[i], 0, wj(i, j, na))),
                  pl.BlockSpec((None, None, tf, d), lambda i, j, te, na: (layer_idx, te[i], wj(i, j, na), 0))],
        out_specs=pl.BlockSpec(io_block, lambda i, j, te, na: (i, 0)),
        scratch_shapes=[pltpu.VMEM((tm, d), BF16), pltpu.VMEM((tm, d), F32)],
    )
    return pl.pallas_call(
        functools.partial(_ffn_body, residual=residual, row_tiles=row_tiles),
        grid_spec=grid_spec,
        out_shape=jax.ShapeDtypeStruct(rows.shape, F32),
        compiler_params=_cp("parallel", "arbitrary"),
        name="swiglu_res" if residual else "swiglu_moe",
    )(tile_expert, n_active, rows, g, wg_all, wu_all, wd_all)


def _route_body(x_ref, g_ref, wr_ref, o_ref):
    h = _rmsnorm(x_ref[...], g_ref[...])
    logits = _fdot(h, wr_ref[...])
    lane = lax.broadcasted_iota(jnp.int32, logits.shape, 1)
    lane_f = lane.astype(F32)
    neg = -jnp.inf
    logits = jnp.where(lane < N_EXPERTS, logits, neg)
    m1 = jnp.max(logits, axis=-1, keepdims=True)
    i1 = jnp.min(jnp.where(logits == m1, lane_f, 128.0), axis=-1, keepdims=True)
    rest = jnp.where(lane_f == i1, neg, logits)
    m2 = jnp.max(rest, axis=-1, keepdims=True)
    i2 = jnp.min(jnp.where(rest == m2, lane_f, 128.0), axis=-1, keepdims=True)
    e = jnp.exp(m2 - m1)
    p1 = 1.0 / (1.0 + e)
    p2 = e / (1.0 + e)
    out = jnp.where(lane == 0, i1, 0.0)
    out = jnp.where(lane == 1, i2, out)
    out = jnp.where(lane == 2, p1, out)
    out = jnp.where(lane == 3, p2, out)
    o_ref[...] = out


def _route(x2, g, wr_pad):
    n, d = x2.shape
    tm = min(TM_ROUTE, n)
    return pl.pallas_call(
        _route_body,
        grid=(n // tm,),
        in_specs=[pl.BlockSpec((tm, d), lambda i: (i, 0)),
                  pl.BlockSpec((1, d), lambda i: (0, 0)),
                  pl.BlockSpec((d, 128), lambda i: (0, 0))],
        out_specs=pl.BlockSpec((tm, 128), lambda i: (i, 0)),
        out_shape=jax.ShapeDtypeStruct((n, 128), F32),
        compiler_params=_cp("parallel"),
        name="router",
    )(x2, g, wr_pad)


DMA_UNROLL = 8


def _row_tile(ref, row):
    return ref.at[pl.ds(pl.multiple_of(row * ROW_TILE, ROW_TILE), ROW_TILE)]


def _wait_rows(src, dst, n_rows, sem):
    span = pl.ds(0, n_rows * ROW_TILE)
    pltpu.make_async_copy(src.at[span], dst.at[span], sem).wait()


def _dispatch_body(s1_ref, s2_ref, x_ref, zero_hbm, o_hbm, sems):
    del zero_hbm
    tm = s1_ref.shape[-1]

    def issue(it, carry):
        for u in range(DMA_UNROLL):
            r = it * DMA_UNROLL + u
            src = _row_tile(x_ref, r)
            pltpu.make_async_copy(src, _row_tile(o_hbm, s1_ref[0, 0, r]), sems.at[0]).start(priority=0)
            pltpu.make_async_copy(src, _row_tile(o_hbm, s2_ref[0, 0, r]), sems.at[1]).start(priority=1)
        return carry

    lax.fori_loop(0, tm // DMA_UNROLL, issue, 0)
    _wait_rows(x_ref, o_hbm, tm, sems.at[0])
    _wait_rows(x_ref, o_hbm, tm, sems.at[1])


def _dispatch(x_tiles, slot1, slot2, n_slots):
    n = slot1.shape[0]
    tm = min(TM_DISPATCH, n)
    sspec = pl.BlockSpec((1, 1, tm), lambda i: (i, 0, 0), memory_space=pltpu.SMEM)
    zeros = jnp.zeros((n_slots * ROW_TILE, 128), F32)
    return pl.pallas_call(
        _dispatch_body,
        grid=(n // tm,),
        in_specs=[sspec, sspec, pl.BlockSpec((tm * ROW_TILE, 128), lambda i: (i, 0)),
                  pl.BlockSpec(memory_space=pl.ANY)],
        out_specs=pl.BlockSpec(memory_space=pl.ANY),
        out_shape=jax.ShapeDtypeStruct(zeros.shape, F32),
        scratch_shapes=[pltpu.SemaphoreType.DMA((2,))],
        input_output_aliases={3: 0},
        compiler_params=_cp("arbitrary"),
        name="moe_dispatch",
    )(slot1.reshape(n // tm, 1, tm), slot2.reshape(n // tm, 1, tm), x_tiles, zeros)


def _combine_body(s1_ref, s2_ref, s1n_ref, s2n_ref, y_hbm, x_ref, info_ref, g_ref, o_ref,
                  y1_ref, y2_ref, sems, *, final_norm):
    i = pl.program_id(0)
    tm = x_ref.shape[0]
    buf = lax.rem(i, 2)

    def fetch(i1_ref, i2_ref, b):
        def issue(it, carry):
            for u in range(DMA_UNROLL):
                r = it * DMA_UNROLL + u
                pltpu.make_async_copy(_row_tile(y_hbm, i1_ref[0, 0, r]), _row_tile(y1_ref.at[b], r),
                                      sems.at[0, b]).start(priority=0)
                pltpu.make_async_copy(_row_tile(y_hbm, i2_ref[0, 0, r]), _row_tile(y2_ref.at[b], r),
                                      sems.at[1, b]).start(priority=1)
            return carry

        lax.fori_loop(0, tm // DMA_UNROLL, issue, 0)

    @pl.when(i == 0)
    def _():
        fetch(s1_ref, s2_ref, 0)

    @pl.when(i + 1 < pl.num_programs(0))
    def _():
        fetch(s1n_ref, s2n_ref, 1 - buf)

    _wait_rows(y_hbm, y1_ref.at[buf], tm, sems.at[0, buf])
    _wait_rows(y_hbm, y2_ref.at[buf], tm, sems.at[1, buf])
    p1 = info_ref[:, 2:3]
    p2 = info_ref[:, 3:4]
    for s in range(ROW_TILE):
        cols = slice(s * 128, (s + 1) * 128)
        chunk = pl.ds(s, tm, stride=ROW_TILE)
        o_ref[:, cols] = x_ref[:, cols] + p1 * y1_ref[buf, chunk, :] + p2 * y2_ref[buf, chunk, :]
    if final_norm:
        o_ref[...] = _rmsnorm(o_ref[...], g_ref[...])


def _combine(y_tiles, x2, info, slot1, slot2, g_final, final_norm):
    n, d = x2.shape
    tm = min(TM_COMBINE, n)
    nt = n // tm
    cur = pl.BlockSpec((1, 1, tm), lambda i: (i, 0, 0), memory_space=pltpu.SMEM)
    nxt = pl.BlockSpec((1, 1, tm), lambda i: (jnp.minimum(i + 1, nt - 1), 0, 0), memory_space=pltpu.SMEM)
    s1 = slot1.reshape(nt, 1, tm)
    s2 = slot2.reshape(nt, 1, tm)
    return pl.pallas_call(
        functools.partial(_combine_body, final_norm=final_norm),
        grid=(nt,),
        in_specs=[cur, cur, nxt, nxt,
                  pl.BlockSpec(memory_space=pl.ANY),
                  pl.BlockSpec((tm, d), lambda i: (i, 0)),
                  pl.BlockSpec((tm, 128), lambda i: (i, 0)),
                  pl.BlockSpec((1, d), lambda i: (0, 0))],
        out_specs=pl.BlockSpec((tm, d), lambda i: (i, 0)),
        out_shape=jax.ShapeDtypeStruct((n, d), F32),
        scratch_shapes=[pltpu.VMEM((2, tm * ROW_TILE, 128), F32), pltpu.VMEM((2, tm * ROW_TILE, 128), F32),
                        pltpu.SemaphoreType.DMA((2, 2))],
        compiler_params=_cp("arbitrary"),
        name="moe_combine",
    )(s1, s2, s1, s2, y_tiles, x2, info, g_final)


def _dispatch_plan(info, tm):
    n = info.shape[0]
    experts = info[:, 0:2].astype(jnp.int32)
    flat = experts.reshape(-1)
    onehot = (flat[:, None] == jnp.arange(N_EXPERTS, dtype=jnp.int32)[None, :]).astype(jnp.int32)
    csum = jnp.cumsum(onehot, axis=0)
    rank = jnp.sum(csum * onehot, axis=1) - 1
    counts = csum[-1]
    padded = ((counts + tm - 1) // tm) * tm
    ends = jnp.cumsum(padded)
    starts = ends - padded
    slot = (jnp.sum(starts[None, :] * onehot, axis=1) + rank).astype(jnp.int32)
    n_slots = 2 * n + N_EXPERTS * tm
    tile_start = jnp.arange(n_slots // tm, dtype=jnp.int32) * tm
    tile_expert = jnp.minimum(
        jnp.sum((tile_start[:, None] >= ends[None, :]).astype(jnp.int32), axis=1), N_EXPERTS - 1)
    n_active = (ends[-1] // tm).astype(jnp.int32).reshape(1)
    slots = slot.reshape(n, 2)
    return n_slots, tile_expert.astype(jnp.int32), n_active, slots[:, 0], slots[:, 1]


def _prep_w_in(w_in):
    w = lax.optimization_barrier(w_in.astype(BF16))
    sl = lambda a, b: w[:, :, a:b]
    parts = [sl(4872, 8968), sl(0, 1024), sl(3840, 4864), sl(1024, 1536), sl(1536, 3072),
             sl(3328, 3840), sl(3072, 3328), sl(4864, 4872),
             jnp.zeros(w.shape[:2] + (Z_WIDTH - w.shape[2],), BF16)]
    return jnp.concatenate(parts, axis=-1)


def kernel(x, norm_mix_g, w_in, a_ln_g, a_ln_b, a_w_s, a_b_s, b_w_pool, b_scale, c_mix, c_w0, c_w2, c_a0, c_a2, c_g2, c_k_k, c_k_a, c_r_k, c_ln_g, c_ln_b, d_conv_w, d_conv_b, d_dt_bias, d_a_log, d_skip, d_norm_g, w_branch, w_out, norm_ffn_g, ffn_w_gate, ffn_w_up, ffn_w_down, moe_router, moe_w_gate, moe_w_up, moe_w_down, norm_final_g):
    batch, seq, d = x.shape
    depth = w_in.shape[0]
    assert depth % 2 == 0, "the final rmsnorm is fused into the last routed layer's combine"
    n = batch * seq
    x2 = x.reshape(n, d)

    w_in_p = _prep_w_in(w_in)
    wb_all = w_branch.astype(BF16)
    wo_all = w_out.astype(BF16)
    ffn_wg = ffn_w_gate.astype(BF16)[:, None]
    ffn_wu = ffn_w_up.astype(BF16)[:, None]
    ffn_wd = ffn_w_down.astype(BF16)[:, None]
    moe_wg = moe_w_gate.astype(BF16)
    moe_wu = moe_w_up.astype(BF16)
    moe_wd = moe_w_down.astype(BF16)

    row = lambda v: v.reshape(1, -1)
    pad_lanes = lambda v: jnp.pad(v, (0, 128 - v.shape[0])).reshape(1, 128)
    dense_tiles = n // min(TM_FFN, n)
    dense_te = jnp.zeros((dense_tiles,), jnp.int32)
    dense_na = jnp.full((1,), dense_tiles, jnp.int32)

    for l in range(depth):
        z = _in_proj(x2, row(norm_mix_g[l]), w_in_p, l)
        y_a = _mixer_a(z, row(a_ln_g[l]), row(a_ln_b[l]), a_w_s[l], a_b_s[l].T, seq)
        y_b = _mixer_b(z, b_w_pool[l], row(b_scale[l]), batch, seq)
        w2p = jnp.pad(c_w2[l], ((0, 192), (0, 0)))
        a2p = jnp.pad(c_a2[l], ((64, 128), (0, 0)))
        g2p = jnp.pad(c_g2[l], ((128, 0), (0, 0)))
        y_c = _mixer_c(z, row(c_mix[l]), row(c_w0[l]), w2p, row(c_a0[l]), a2p, g2p,
                       row(c_k_k[l]), row(c_k_a[l]), row(c_r_k[l]), row(c_ln_g[l]), row(c_ln_b[l]),
                       batch, seq)
        y_d = _mixer_d(z, d_conv_w[l], row(d_conv_b[l]), pad_lanes(d_dt_bias[l]),
                       pad_lanes(d_a_log[l]), row(jnp.repeat(d_skip[l], HEAD)), row(d_norm_g[l]),
                       batch, seq)
        routed = l % 2 == 1
        merged = _merge((y_a, y_b, y_c, y_d), z, x2, wb_all, wo_all, l, routed)
        x2 = merged[0]

        g_ffn = row(norm_ffn_g[l])
        if l % 2 == 0:
            x2 = _ffn(x2, g_ffn, ffn_wg, ffn_wu, ffn_wd, l // 2, dense_te, dense_na,
                      min(TM_FFN, n), True, False)
        else:
            i = l // 2
            wr_pad = jnp.pad(moe_router[i], ((0, 0), (0, 128 - N_EXPERTS)))
            info = _route(x2, g_ffn, wr_pad)
            n_slots, tile_expert, n_active, slot1, slot2 = _dispatch_plan(info, TM_MOE)
            rows = _dispatch(merged[1], slot1, slot2, n_slots)
            y_tiles = _ffn(rows, g_ffn, moe_wg, moe_wu, moe_wd, i, tile_expert, n_active,
                           TM_MOE, False, True)
            x2 = _combine(y_tiles, x2, info, slot1, slot2, row(norm_final_g), l == depth - 1)
    return x2.reshape(batch, seq, d)
```
